```python
import math
import jax, jax.numpy as jnp
from jax import lax
import numpy as np

D_MODEL = 1024
BATCH = 16
SEQ = 2048
DEPTH = 4

CHUNK = 64
N_MIXERS = 2
N_HEADS_A = 16
HEAD_DIM_A = D_MODEL // N_HEADS_A
LEFT_CHUNKS = 8
BAND = (LEFT_CHUNKS + 1) * CHUNK
REL_CLIP = 128
N_HEADS_B = 16
HEAD_DIM_B = D_MODEL // N_HEADS_B
SB_BLOCK = 128
D_FF = 2816
ALPHA = (2.0 * DEPTH) ** 0.25
BETA = (8.0 * DEPTH) ** -0.25
LN_EPS = 1e-5
N_A_LAYERS = (DEPTH + N_MIXERS - 1) // N_MIXERS
N_B_LAYERS = DEPTH // N_MIXERS

kernel_name = "hybrid_chunked_relbias_stickbreaking_macaron_deepnorm"


def layer_norm(x, g, b):
    xf = x.astype(jnp.float32)
    mu = jnp.mean(xf, axis=-1, keepdims=True)
    var = jnp.mean(jnp.square(xf - mu), axis=-1, keepdims=True)
    y = (xf - mu) * lax.rsqrt(var + LN_EPS)
    return (y * g.astype(jnp.float32) + b.astype(jnp.float32)).astype(x.dtype)


def swiglu(x, w_gate, w_up, w_down):
    return (jax.nn.silu(x @ w_gate) * (x @ w_up)) @ w_down


def rel_bias_band(rel_table):
    i = np.arange(CHUNK)[:, None]
    j = np.arange(BAND)[None, :]
    rel = (np.clip(i - j + LEFT_CHUNKS * CHUNK, -REL_CLIP, REL_CLIP) + REL_CLIP).astype(np.int32)
    return jnp.transpose(rel_table[rel], (2, 0, 1))


def chunked_rel_attention(x, w_qkv, w_o, rel_table):
    B, S, _ = x.shape
    n_chunks = S // CHUNK
    qkv = (x @ w_qkv).reshape(B, S, 3, N_HEADS_A, HEAD_DIM_A)
    q, k, v = qkv[:, :, 0], qkv[:, :, 1], qkv[:, :, 2]
    pad = LEFT_CHUNKS * CHUNK
    k_pad = jnp.pad(k, ((0, 0), (pad, 0), (0, 0), (0, 0)))
    v_pad = jnp.pad(v, ((0, 0), (pad, 0), (0, 0), (0, 0)))
    bias = rel_bias_band(rel_table).astype(jnp.float32)
    key_offset = jnp.arange(BAND) - pad
    scale = HEAD_DIM_A ** -0.5

    def one_chunk(c):
        start = c * CHUNK
        q_c = lax.dynamic_slice_in_dim(q, start, CHUNK, axis=1)
        k_c = lax.dynamic_slice_in_dim(k_pad, start, BAND, axis=1)
        v_c = lax.dynamic_slice_in_dim(v_pad, start, BAND, axis=1)
        s = jnp.einsum('bqhd,bkhd->bhqk', q_c, k_c,
                       preferred_element_type=jnp.float32) * scale + bias
        valid = (start + key_offset) >= 0
        s = jnp.where(valid[None, None, None, :], s, -jnp.inf)
        p = jax.nn.softmax(s, axis=-1).astype(v_c.dtype)
        return jnp.einsum('bhqk,bkhd->bqhd', p, v_c)

    out = lax.map(one_chunk, jnp.arange(n_chunks))
    out = jnp.transpose(out, (1, 0, 2, 3, 4)).reshape(B, S, D_MODEL)
    return out @ w_o


def stick_breaking_attention(x, w_qkv, w_o):
    B, S, _ = x.shape
    qkv = (x @ w_qkv).reshape(B, S, 3, N_HEADS_B, HEAD_DIM_B)
    q, k, v = qkv[:, :, 0], qkv[:, :, 1], qkv[:, :, 2]
    scale = HEAD_DIM_B ** -0.5
    outs = []
    for blk in range(S // SB_BLOCK):
        q0 = blk * SB_BLOCK
        kv_len = q0 + SB_BLOCK
        q_b = q[:, q0:kv_len]
        k_b = k[:, :kv_len]
        v_b = v[:, :kv_len]
        z = jnp.einsum('bqhd,bkhd->bhqk', q_b, k_b,
                       preferred_element_type=jnp.float32) * scale
        t_pos = q0 + jnp.arange(SB_BLOCK)[:, None]
        s_pos = jnp.arange(kv_len)[None, :]
        causal = s_pos < t_pos
        log_beta = jnp.where(causal, jax.nn.log_sigmoid(z), -jnp.inf)
        log_1m_beta = jnp.where(causal, jax.nn.log_sigmoid(-z), 0.0)
        stick = lax.cumsum(log_1m_beta, axis=log_1m_beta.ndim - 1, reverse=True) - log_1m_beta
        a = jnp.exp(log_beta + stick).astype(v_b.dtype)
        outs.append(jnp.einsum('bhqk,bkhd->bqhd', a, v_b))
    out = jnp.concatenate(outs, axis=1).reshape(B, S, D_MODEL)
    return out @ w_o


def setup_inputs(seed: int = 0) -> dict:
    key = jax.random.key(seed)
    ks = jax.random.split(key, 16)
    d_sc = D_MODEL ** -0.5
    f_sc = D_FF ** -0.5
    x = jax.random.normal(ks[0], (BATCH, SEQ, D_MODEL), jnp.float32)

    def qkv_weights(k, n):
        kq, kk, kv = jax.random.split(k, 3)
        wq = jax.random.normal(kq, (n, D_MODEL, D_MODEL), jnp.float32) * d_sc
        wk = jax.random.normal(kk, (n, D_MODEL, D_MODEL), jnp.float32) * d_sc
        wv = jax.random.normal(kv, (n, D_MODEL, D_MODEL), jnp.float32) * (d_sc * BETA)
        return jnp.concatenate([wq, wk, wv], axis=-1)

    w_qkv_a = qkv_weights(ks[1], N_A_LAYERS)
    w_o_a = jax.random.normal(ks[2], (N_A_LAYERS, D_MODEL, D_MODEL), jnp.float32) * (d_sc * BETA)
    rel_bias = jax.random.normal(ks[3], (2 * REL_CLIP + 1, N_HEADS_A), jnp.float32) * 0.5
    w_qkv_b = qkv_weights(ks[4], N_B_LAYERS)
    w_o_b = jax.random.normal(ks[5], (N_B_LAYERS, D_MODEL, D_MODEL), jnp.float32) * (d_sc * BETA)
    ffn_w_gate = jax.random.normal(ks[6], (DEPTH, 2, D_MODEL, D_FF), jnp.float32) * d_sc
    ffn_w_up = jax.random.normal(ks[7], (DEPTH, 2, D_MODEL, D_FF), jnp.float32) * d_sc
    ffn_w_down = jax.random.normal(ks[8], (DEPTH, 2, D_FF, D_MODEL), jnp.float32) * (f_sc * BETA)
    ln_g = 1.0 + 0.02 * jax.random.normal(ks[9], (DEPTH, 3, D_MODEL), jnp.float32)
    ln_b = 0.02 * jax.random.normal(ks[10], (DEPTH, 3, D_MODEL), jnp.float32)
    return {"x": x, "w_qkv_a": w_qkv_a, "w_o_a": w_o_a, "rel_bias": rel_bias,
            "w_qkv_b": w_qkv_b, "w_o_b": w_o_b, "ffn_w_gate": ffn_w_gate,
            "ffn_w_up": ffn_w_up, "ffn_w_down": ffn_w_down, "ln_g": ln_g, "ln_b": ln_b}


def reference(x, w_qkv_a, w_o_a, rel_bias, w_qkv_b, w_o_b, ffn_w_gate, ffn_w_up,
              ffn_w_down, ln_g, ln_b):
    for i in range(DEPTH):
        h = swiglu(x, ffn_w_gate[i, 0], ffn_w_up[i, 0], ffn_w_down[i, 0])
        x = layer_norm(ALPHA * x + 0.5 * h, ln_g[i, 0], ln_b[i, 0])
        j = i // N_MIXERS
        if i % N_MIXERS == 0:
            y = chunked_rel_attention(x, w_qkv_a[j], w_o_a[j], rel_bias)
        else:
            y = stick_breaking_attention(x, w_qkv_b[j], w_o_b[j])
        x = layer_norm(ALPHA * x + y, ln_g[i, 1], ln_b[i, 1])
        h = swiglu(x, ffn_w_gate[i, 1], ffn_w_up[i, 1], ffn_w_down[i, 1])
        x = layer_norm(ALPHA * x + 0.5 * h, ln_g[i, 2], ln_b[i, 2])
    return x
```

```python
import functools

import numpy as np
import jax
import jax.numpy as jnp
from jax import lax
from jax.experimental import pallas as pl
from jax.experimental.pallas import tpu as pltpu

D_MODEL = 1024
DEPTH = 4
N_HEADS = 16
HEAD_DIM = D_MODEL // N_HEADS
HEADS_PER_STEP = 2
HEAD_PAIR_WIDTH = HEADS_PER_STEP * HEAD_DIM
N_HEAD_PAIRS = N_HEADS // HEADS_PER_STEP
D_FF = 2816
CHUNK = 64
LEFT_CHUNKS = 8
LEFT_CONTEXT = LEFT_CHUNKS * CHUNK
REL_CLIP = 128
ALPHA = (2.0 * DEPTH) ** 0.25
LN_EPS = 1e-5

ROW_TILE = 512
A_Q_BLOCK = 128
A_KEY_WINDOW = LEFT_CONTEXT + A_Q_BLOCK
B_Q_BLOCK = 128
B_KEY_BLOCK = 256
VMEM_LIMIT_BYTES = 56 * 1024 * 1024

_BF16 = jnp.bfloat16
_F32 = jnp.float32


def _compiler_params(n_grid_dims):
    return pltpu.CompilerParams(
        dimension_semantics=("arbitrary",) * n_grid_dims,
        vmem_limit_bytes=VMEM_LIMIT_BYTES,
    )


def _resident(shape):
    zeros = (0,) * len(shape)
    return pl.BlockSpec(shape, lambda *_: zeros, pipeline_mode=pl.Buffered(1))


def _layer_norm_rows(y, gain, bias):
    mu = jnp.mean(y, axis=-1, keepdims=True)
    d = y - mu
    var = jnp.mean(d * d, axis=-1, keepdims=True)
    return d * lax.rsqrt(var + LN_EPS) * gain + bias


def _ffn_ln_kernel(x_ref, wg_ref, wu_ref, wd_ref, gain_ref, bias_ref, o_ref):
    x = x_ref[...]
    xb = x.astype(_BF16)
    gate = jnp.dot(xb, wg_ref[...], preferred_element_type=_F32)
    up = jnp.dot(xb, wu_ref[...], preferred_element_type=_F32)
    hidden = (gate * jax.nn.sigmoid(gate) * up).astype(_BF16)
    h = jnp.dot(hidden, wd_ref[...], preferred_element_type=_F32)
    o_ref[...] = _layer_norm_rows(ALPHA * x + 0.5 * h, gain_ref[...], bias_ref[...])


def _ffn_ln(x2d, wg, wu, wd, gain, bias):
    n_rows, d = x2d.shape
    f = wg.shape[1]
    row_spec = pl.BlockSpec((ROW_TILE, d), lambda i: (i, 0))
    return pl.pallas_call(
        _ffn_ln_kernel,
        grid=(n_rows // ROW_TILE,),
        in_specs=[row_spec, _resident((d, f)), _resident((d, f)), _resident((f, d)),
                  _resident((1, d)), _resident((1, d))],
        out_specs=row_spec,
        out_shape=jax.ShapeDtypeStruct((n_rows, d), _F32),
        compiler_params=_compiler_params(1),
        name="ffn_ln",
    )(x2d, wg, wu, wd, gain, bias)


def _qkv_kernel(x_ref, w_ref, o_ref):
    o_ref[...] = jnp.dot(x_ref[...].astype(_BF16), w_ref[...],
                         preferred_element_type=_F32).astype(o_ref.dtype)


def _qkv_proj(x2d, w_qkv):
    n_rows, d = x2d.shape
    n_out = w_qkv.shape[1]
    return pl.pallas_call(
        _qkv_kernel,
        grid=(n_rows // ROW_TILE,),
        in_specs=[pl.BlockSpec((ROW_TILE, d), lambda i: (i, 0)), _resident((d, n_out))],
        out_specs=pl.BlockSpec((ROW_TILE, n_out), lambda i: (i, 0)),
        out_shape=jax.ShapeDtypeStruct((n_rows, n_out), _BF16),
        compiler_params=_compiler_params(1),
        name="qkv_proj",
    )(x2d, w_qkv)


def _proj_ln_kernel(a_ref, x_ref, w_ref, gain_ref, bias_ref, o_ref):
    y = jnp.dot(a_ref[...], w_ref[...], preferred_element_type=_F32)
    o_ref[...] = _layer_norm_rows(ALPHA * x_ref[...] + y, gain_ref[...], bias_ref[...])


def _proj_ln(attn2d, x2d, w_o, gain, bias):
    n_rows, d = x2d.shape
    row_spec = pl.BlockSpec((ROW_TILE, d), lambda i: (i, 0))
    return pl.pallas_call(
        _proj_ln_kernel,
        grid=(n_rows // ROW_TILE,),
        in_specs=[row_spec, row_spec, _resident((d, d)), _resident((1, d)), _resident((1, d))],
        out_specs=row_spec,
        out_shape=jax.ShapeDtypeStruct((n_rows, d), _F32),
        compiler_params=_compiler_params(1),
        name="proj_ln",
    )(attn2d, x2d, w_o, gain, bias)


def _head_lane_masks(dtype):
    lane = lax.broadcasted_iota(jnp.int32, (1, HEAD_PAIR_WIDTH), 1)
    first = lane < HEAD_DIM
    return [first.astype(dtype), (~first).astype(dtype)], first


def _head_pair_specs(seq):
    def spec(offset):
        return pl.BlockSpec((None, seq, HEAD_PAIR_WIDTH), lambda b, hp: (b, 0, offset + hp))
    return [spec(0), spec(N_HEAD_PAIRS), spec(2 * N_HEAD_PAIRS)], spec(0)


def _attn_a_kernel(q_ref, k_ref, v_ref, bias_ref, o_ref, kpad_ref, vpad_ref):
    seq = q_ref.shape[0]
    zeros = jnp.zeros((LEFT_CONTEXT, HEAD_PAIR_WIDTH), _BF16)
    kpad_ref[0:LEFT_CONTEXT, :] = zeros
    vpad_ref[0:LEFT_CONTEXT, :] = zeros
    kpad_ref[LEFT_CONTEXT:, :] = k_ref[...]
    vpad_ref[LEFT_CONTEXT:, :] = v_ref[...]
    head_masks, first_head = _head_lane_masks(_BF16)
    window_idx = lax.broadcasted_iota(jnp.int32, (A_Q_BLOCK, A_KEY_WINDOW), 1)

    def q_block(qi, carry):
        q0 = pl.multiple_of(qi * A_Q_BLOCK, A_Q_BLOCK)
        q2 = q_ref[pl.ds(q0, A_Q_BLOCK), :]
        k_win = kpad_ref[pl.ds(q0, A_KEY_WINDOW), :]
        v_win = vpad_ref[pl.ds(q0, A_KEY_WINDOW), :]
        in_sequence = window_idx >= (LEFT_CONTEXT - q0)
        outs = []
        for h in range(HEADS_PER_STEP):
            s = lax.dot_general(q2 * head_masks[h], k_win, (((1,), (1,)), ((), ())),
                                preferred_element_type=_F32)
            s = jnp.where(in_sequence, s + bias_ref[h], -jnp.inf)
            m = jnp.max(s, axis=-1, keepdims=True)
            p = jnp.exp(s - m)
            denom = jnp.sum(p, axis=-1, keepdims=True)
            o = jnp.dot(p.astype(_BF16), v_win, preferred_element_type=_F32)
            outs.append(o / denom)
        o_ref[pl.ds(q0, A_Q_BLOCK), :] = jnp.where(first_head, outs[0], outs[1]).astype(o_ref.dtype)
        return carry

    lax.fori_loop(0, seq // A_Q_BLOCK, q_block, 0)


def _band_bias(rel_table):
    i = np.arange(A_Q_BLOCK)[:, None]
    j = np.arange(A_KEY_WINDOW)[None, :]
    rel = (np.clip(i - j + LEFT_CONTEXT, -REL_CLIP, REL_CLIP) + REL_CLIP).astype(np.int32)
    q_chunk, k_chunk = i // CHUNK, j // CHUNK
    in_band = (k_chunk >= q_chunk) & (k_chunk <= q_chunk + LEFT_CHUNKS)
    bias = jnp.transpose(rel_table[rel], (2, 0, 1)).astype(_F32)
    return jnp.where(in_band[None], bias, -jnp.inf)


def _attn_a(qkv3d, band_bias):
    batch, seq, _ = qkv3d.shape
    in_specs, out_spec = _head_pair_specs(seq)
    bias_spec = pl.BlockSpec((HEADS_PER_STEP, A_Q_BLOCK, A_KEY_WINDOW), lambda b, hp: (hp, 0, 0))
    return pl.pallas_call(
        _attn_a_kernel,
        grid=(batch, N_HEAD_PAIRS),
        in_specs=in_specs + [bias_spec],
        out_specs=out_spec,
        out_shape=jax.ShapeDtypeStruct((batch, seq, D_MODEL), _BF16),
        scratch_shapes=[pltpu.VMEM((seq + LEFT_CONTEXT, HEAD_PAIR_WIDTH), _BF16)] * 2,
        compiler_params=_compiler_params(2),
        name="attn_band",
    )(qkv3d, qkv3d, qkv3d, band_bias)


def _stick_block(qh, k_blk, v_blk, tri, tail, causal):
    z = lax.dot_general(qh, k_blk, (((1,), (1,)), ((), ())), preferred_element_type=_F32)
    softplus_tail = jnp.log(1.0 + jnp.exp(-jnp.abs(z)))
    log_beta = jnp.minimum(z, 0.0) - softplus_tail
    log_1m_beta = jnp.minimum(-z, 0.0) - softplus_tail
    if causal is not None:
        log_beta = jnp.where(causal, log_beta, -jnp.inf)
        log_1m_beta = jnp.where(causal, log_1m_beta, 0.0)
    hi = log_1m_beta.astype(_BF16)
    lo = (log_1m_beta - hi.astype(_F32)).astype(_BF16)
    later = (jnp.dot(hi, tri, preferred_element_type=_F32)
             + jnp.dot(lo, tri, preferred_element_type=_F32))
    a = jnp.exp(log_beta + later + tail)
    out = jnp.dot(a.astype(_BF16), v_blk, preferred_element_type=_F32)
    return out, tail + jnp.sum(log_1m_beta, axis=-1, keepdims=True)


def _attn_b_kernel(q_ref, k_ref, v_ref, o_ref):
    seq = q_ref.shape[0]
    head_masks, first_head = _head_lane_masks(_BF16)
    key_j = lax.broadcasted_iota(jnp.int32, (B_KEY_BLOCK, B_KEY_BLOCK), 0)
    key_s = lax.broadcasted_iota(jnp.int32, (B_KEY_BLOCK, B_KEY_BLOCK), 1)
    tri = (key_j > key_s).astype(_BF16)
    q_row = lax.broadcasted_iota(jnp.int32, (B_Q_BLOCK, B_KEY_BLOCK), 0)
    k_col = lax.broadcasted_iota(jnp.int32, (B_Q_BLOCK, B_KEY_BLOCK), 1)
    q_blocks_per_key_block = B_KEY_BLOCK // B_Q_BLOCK

    def q_block(qi, carry):
        q0 = pl.multiple_of(qi * B_Q_BLOCK, B_Q_BLOCK)
        q2 = q_ref[pl.ds(q0, B_Q_BLOCK), :]
        diag = qi // q_blocks_per_key_block
        d0 = pl.multiple_of(diag * B_KEY_BLOCK, B_KEY_BLOCK)
        causal = k_col < q_row + (q0 - d0)
        outs = []
        for h in range(HEADS_PER_STEP):
            qh = q2 * head_masks[h]
            tail0 = jnp.zeros((B_Q_BLOCK, 1), _F32)
            acc, tail = _stick_block(qh, k_ref[pl.ds(d0, B_KEY_BLOCK), :],
                                     v_ref[pl.ds(d0, B_KEY_BLOCK), :], tri, tail0, causal)

            def key_block(t, state, qh=qh):
                acc, tail = state
                k0 = pl.multiple_of((diag - 1 - t) * B_KEY_BLOCK, B_KEY_BLOCK)
                out, tail = _stick_block(qh, k_ref[pl.ds(k0, B_KEY_BLOCK), :],
                                         v_ref[pl.ds(k0, B_KEY_BLOCK), :], tri, tail, None)
                return acc + out, tail

            acc, _ = lax.fori_loop(0, diag, key_block, (acc, tail))
            outs.append(acc)
        o_ref[pl.ds(q0, B_Q_BLOCK), :] = jnp.where(first_head, outs[0], outs[1]).astype(o_ref.dtype)
        return carry

    lax.fori_loop(0, seq // B_Q_BLOCK, q_block, 0)


def _attn_b(qkv3d):
    batch, seq, _ = qkv3d.shape
    in_specs, out_spec = _head_pair_specs(seq)
    return pl.pallas_call(
        _attn_b_kernel,
        grid=(batch, N_HEAD_PAIRS),
        in_specs=in_specs,
        out_specs=out_spec,
        out_shape=jax.ShapeDtypeStruct((batch, seq, D_MODEL), _BF16),
        compiler_params=_compiler_params(2),
        name="attn_stick",
    )(qkv3d, qkv3d, qkv3d)


def _scaled_qkv_weights(w_qkv):
    scale = HEAD_DIM ** -0.5
    col_scale = jnp.concatenate([jnp.full((D_MODEL,), scale, _F32), jnp.ones((2 * D_MODEL,), _F32)])
    return (w_qkv * col_scale).astype(_BF16)


def kernel(x, w_qkv_a, w_o_a, rel_bias, w_qkv_b, w_o_b, ffn_w_gate, ffn_w_up, ffn_w_down, ln_g, ln_b):
    batch, seq, d = x.shape
    assert d == D_MODEL and (batch * seq) % ROW_TILE == 0
    assert seq % B_KEY_BLOCK == 0 and seq % A_Q_BLOCK == 0
    assert HEAD_DIM ** -0.5 == 0.125

    wg = ffn_w_gate.astype(_BF16)
    wu = ffn_w_up.astype(_BF16)
    wd = ffn_w_down.astype(_BF16)
    wqkv_a = _scaled_qkv_weights(w_qkv_a)
    wqkv_b = _scaled_qkv_weights(w_qkv_b)
    wo_a = w_o_a.astype(_BF16)
    wo_b = w_o_b.astype(_BF16)
    band_bias = _band_bias(rel_bias)
    gains = ln_g.reshape(DEPTH, 3, 1, D_MODEL)
    biases = ln_b.reshape(DEPTH, 3, 1, D_MODEL)

    x2d = x.reshape(batch * seq, d)
    for i in range(DEPTH):
        x2d = _ffn_ln(x2d, wg[i, 0], wu[i, 0], wd[i, 0], gains[i, 0], biases[i, 0])
        j = i // 2
        if i % 2 == 0:
            qkv = _qkv_proj(x2d, wqkv_a[j]).reshape(batch, seq, 3 * d)
            attn = _attn_a(qkv, band_bias)
            w_o = wo_a[j]
        else:
            qkv = _qkv_proj(x2d, wqkv_b[j]).reshape(batch, seq, 3 * d)
            attn = _attn_b(qkv)
            w_o = wo_b[j]
        x2d = _proj_ln(attn.reshape(batch * seq, d), x2d, w_o, gains[i, 1], biases[i, 1])
        x2d = _ffn_ln(x2d, wg[i, 1], wu[i, 1], wd[i, 1], gains[i, 2], biases[i, 2])
    return x2d.reshape(batch, seq, d)
```

```python
import numpy as np
import jax
import jax.numpy as jnp
from jax import lax
from jax.experimental import pallas as pl
from jax.experimental.pallas import tpu as pltpu

D_MODEL = 1024
DEPTH = 4
N_HEADS = 16
HEAD_DIM = D_MODEL // N_HEADS
HEADS_PER_STEP = 2
HEAD_PAIR_WIDTH = HEADS_PER_STEP * HEAD_DIM
N_HEAD_PAIRS = N_HEADS // HEADS_PER_STEP
D_FF = 2816
CHUNK = 64
LEFT_CHUNKS = 8
LEFT_CONTEXT = LEFT_CHUNKS * CHUNK
REL_CLIP = 128
ALPHA = (2.0 * DEPTH) ** 0.25
LN_EPS = 1e-5

ROW_TILE = 512
A_Q_BLOCK = 128
A_KEY_WINDOW = LEFT_CONTEXT + A_Q_BLOCK
B_Q_BLOCK = 128
B_KEY_BLOCK = 256
B_UNIT_ROWS = HEADS_PER_STEP * B_KEY_BLOCK
VMEM_LIMIT_BYTES = 56 * 1024 * 1024

_BF16 = jnp.bfloat16
_F32 = jnp.float32


def _compiler_params(n_grid_dims):
    return pltpu.CompilerParams(
        dimension_semantics=("arbitrary",) * n_grid_dims,
        vmem_limit_bytes=VMEM_LIMIT_BYTES,
    )


def _resident(shape):
    zeros = (0,) * len(shape)
    return pl.BlockSpec(shape, lambda *_: zeros, pipeline_mode=pl.Buffered(1))


def _layer_norm_rows(y, gain, bias):
    mu = jnp.mean(y, axis=-1, keepdims=True)
    d = y - mu
    var = jnp.mean(d * d, axis=-1, keepdims=True)
    return d * lax.rsqrt(var + LN_EPS) * gain + bias


def _ffn_ln_kernel(x_ref, wg_ref, wu_ref, wd_ref, gain_ref, bias_ref, o_ref):
    x = x_ref[...]
    xb = x.astype(_BF16)
    gate = jnp.dot(xb, wg_ref[...], preferred_element_type=_F32)
    up = jnp.dot(xb, wu_ref[...], preferred_element_type=_F32)
    hidden = (gate * jax.nn.sigmoid(gate) * up).astype(_BF16)
    h = jnp.dot(hidden, wd_ref[...], preferred_element_type=_F32)
    o_ref[...] = _layer_norm_rows(ALPHA * x + 0.5 * h, gain_ref[...], bias_ref[...])


def _ffn_ln(x2d, wg, wu, wd, gain, bias):
    n_rows, d = x2d.shape
    f = wg.shape[1]
    row_spec = pl.BlockSpec((ROW_TILE, d), lambda i: (i, 0))
    return pl.pallas_call(
        _ffn_ln_kernel,
        grid=(n_rows // ROW_TILE,),
        in_specs=[row_spec, _resident((d, f)), _resident((d, f)), _resident((f, d)),
                  _resident((1, d)), _resident((1, d))],
        out_specs=row_spec,
        out_shape=jax.ShapeDtypeStruct((n_rows, d), _F32),
        compiler_params=_compiler_params(1),
        name="ffn_ln",
    )(x2d, wg, wu, wd, gain, bias)


def _qkv_kernel(x_ref, w_ref, o_ref):
    o_ref[...] = jnp.dot(x_ref[...].astype(_BF16), w_ref[...],
                         preferred_element_type=_F32).astype(o_ref.dtype)


def _qkv_proj(x2d, w_qkv):
    n_rows, d = x2d.shape
    n_out = w_qkv.shape[1]
    return pl.pallas_call(
        _qkv_kernel,
        grid=(n_rows // ROW_TILE,),
        in_specs=[pl.BlockSpec((ROW_TILE, d), lambda i: (i, 0)), _resident((d, n_out))],
        out_specs=pl.BlockSpec((ROW_TILE, n_out), lambda i: (i, 0)),
        out_shape=jax.ShapeDtypeStruct((n_rows, n_out), _BF16),
        compiler_params=_compiler_params(1),
        name="qkv_proj",
    )(x2d, w_qkv)


def _proj_ln_kernel(a_ref, x_ref, w_ref, gain_ref, bias_ref, o_ref):
    y = jnp.dot(a_ref[...], w_ref[...], preferred_element_type=_F32)
    o_ref[...] = _layer_norm_rows(ALPHA * x_ref[...] + y, gain_ref[...], bias_ref[...])


def _proj_ln(attn2d, x2d, w_o, gain, bias):
    n_rows, d = x2d.shape
    row_spec = pl.BlockSpec((ROW_TILE, d), lambda i: (i, 0))
    return pl.pallas_call(
        _proj_ln_kernel,
        grid=(n_rows // ROW_TILE,),
        in_specs=[row_spec, row_spec, _resident((d, d)), _resident((1, d)), _resident((1, d))],
        out_specs=row_spec,
        out_shape=jax.ShapeDtypeStruct((n_rows, d), _F32),
        compiler_params=_compiler_params(1),
        name="proj_ln",
    )(attn2d, x2d, w_o, gain, bias)


def _head_lane_masks(dtype):
    lane = lax.broadcasted_iota(jnp.int32, (1, HEAD_PAIR_WIDTH), 1)
    first = lane < HEAD_DIM
    return [first.astype(dtype), (~first).astype(dtype)], first


def _head_pair_specs(seq):
    def spec(offset):
        return pl.BlockSpec((None, seq, HEAD_PAIR_WIDTH), lambda b, hp: (b, 0, offset + hp))
    return [spec(0), spec(N_HEAD_PAIRS), spec(2 * N_HEAD_PAIRS)], spec(0)


def _attn_a_kernel(q_ref, k_ref, v_ref, bias_ref, o_ref, kpad_ref, vpad_ref):
    seq = q_ref.shape[0]
    zeros = jnp.zeros((LEFT_CONTEXT, HEAD_PAIR_WIDTH), _BF16)
    kpad_ref[0:LEFT_CONTEXT, :] = zeros
    vpad_ref[0:LEFT_CONTEXT, :] = zeros
    kpad_ref[LEFT_CONTEXT:, :] = k_ref[...]
    vpad_ref[LEFT_CONTEXT:, :] = v_ref[...]
    head_masks, first_head = _head_lane_masks(_BF16)
    window_idx = lax.broadcasted_iota(jnp.int32, (A_Q_BLOCK, A_KEY_WINDOW), 1)

    def q_block(qi, carry):
        q0 = pl.multiple_of(qi * A_Q_BLOCK, A_Q_BLOCK)
        q2 = q_ref[pl.ds(q0, A_Q_BLOCK), :]
        k_win = kpad_ref[pl.ds(q0, A_KEY_WINDOW), :]
        v_win = vpad_ref[pl.ds(q0, A_KEY_WINDOW), :]
        in_sequence = window_idx >= (LEFT_CONTEXT - q0)
        outs = []
        for h in range(HEADS_PER_STEP):
            s = lax.dot_general(q2 * head_masks[h], k_win, (((1,), (1,)), ((), ())),
                                preferred_element_type=_F32)
            s = jnp.where(in_sequence, s + bias_ref[h], -jnp.inf)
            m = jnp.max(s, axis=-1, keepdims=True)
            p = jnp.exp(s - m)
            denom = jnp.sum(p, axis=-1, keepdims=True)
            o = jnp.dot(p.astype(_BF16), v_win, preferred_element_type=_F32)
            outs.append(o / denom)
        o_ref[pl.ds(q0, A_Q_BLOCK), :] = jnp.where(first_head, outs[0], outs[1]).astype(o_ref.dtype)
        return carry

    lax.fori_loop(0, seq // A_Q_BLOCK, q_block, 0)


def _band_bias(rel_table):
    i = np.arange(A_Q_BLOCK)[:, None]
    j = np.arange(A_KEY_WINDOW)[None, :]
    rel = (np.clip(i - j + LEFT_CONTEXT, -REL_CLIP, REL_CLIP) + REL_CLIP).astype(np.int32)
    q_chunk, k_chunk = i // CHUNK, j // CHUNK
    in_band = (k_chunk >= q_chunk) & (k_chunk <= q_chunk + LEFT_CHUNKS)
    bias = jnp.transpose(rel_table[rel], (2, 0, 1)).astype(_F32)
    return jnp.where(in_band[None], bias, -jnp.inf)


def _attn_a(qkv3d, band_bias):
    batch, seq, _ = qkv3d.shape
    in_specs, out_spec = _head_pair_specs(seq)
    bias_spec = pl.BlockSpec((HEADS_PER_STEP, A_Q_BLOCK, A_KEY_WINDOW), lambda b, hp: (hp, 0, 0))
    return pl.pallas_call(
        _attn_a_kernel,
        grid=(batch, N_HEAD_PAIRS),
        in_specs=in_specs + [bias_spec],
        out_specs=out_spec,
        out_shape=jax.ShapeDtypeStruct((batch, seq, D_MODEL), _BF16),
        scratch_shapes=[pltpu.VMEM((seq + LEFT_CONTEXT, HEAD_PAIR_WIDTH), _BF16)] * 2,
        compiler_params=_compiler_params(2),
        name="attn_band",
    )(qkv3d, qkv3d, qkv3d, band_bias)


def _stick_units(seq):
    n = seq // B_KEY_BLOCK
    return [(g, kj) for kj in range(n - 1, -1, -1) for g in range(kj, n)]


def _attn_b_kernel(q_ref, k_ref, v_ref, o_ref,
                   qs_ref, acc_ref, tail_ref, tri_ref,
                   logit0_ref, logit1_ref, addend0_ref, addend1_ref, a0_ref, a1_ref):
    seq = q_ref.shape[0]
    head_masks, first_head = _head_lane_masks(_BF16)
    logit_refs, addend_refs, a_refs = (logit0_ref, logit1_ref), (addend0_ref, addend1_ref), (a0_ref, a1_ref)

    for qi in range(seq // B_Q_BLOCK):
        q2 = q_ref[qi * B_Q_BLOCK:(qi + 1) * B_Q_BLOCK, :]
        for h in range(HEADS_PER_STEP):
            r0 = (qi * HEADS_PER_STEP + h) * B_Q_BLOCK
            qs_ref[r0:r0 + B_Q_BLOCK, :] = q2 * head_masks[h]

    key_j = lax.broadcasted_iota(jnp.int32, (B_KEY_BLOCK, B_KEY_BLOCK), 0)
    key_s = lax.broadcasted_iota(jnp.int32, (B_KEY_BLOCK, B_KEY_BLOCK), 1)
    tri_ref[...] = (key_j > key_s).astype(_BF16)

    row = lax.broadcasted_iota(jnp.int32, (B_UNIT_ROWS, B_KEY_BLOCK), 0)
    col = lax.broadcasted_iota(jnp.int32, (B_UNIT_ROWS, B_KEY_BLOCK), 1)
    q_in_block = (row // (HEADS_PER_STEP * B_Q_BLOCK)) * B_Q_BLOCK + row % B_Q_BLOCK
    causal = col < q_in_block
    sign_bit = jnp.uint32(0x80000000)

    units = _stick_units(seq)

    def scores(u):
        g, kj = units[u]
        diagonal = g == kj
        rows = slice(g * B_UNIT_ROWS, (g + 1) * B_UNIT_ROWS)
        keys = slice(kj * B_KEY_BLOCK, (kj + 1) * B_KEY_BLOCK)
        z = lax.dot_general(qs_ref[rows, :], k_ref[keys, :], (((1,), (1,)), ((), ())),
                            preferred_element_type=_F32)
        neg_abs = lax.bitcast_convert_type(lax.bitcast_convert_type(z, jnp.uint32) | sign_bit, _F32)
        log_beta = jnp.minimum(z, 0.0) - jnp.log(1.0 + jnp.exp(neg_abs))
        log_1m_beta = log_beta - z
        if diagonal:
            log_beta = jnp.where(causal, log_beta, -jnp.inf)
            log_1m_beta = jnp.where(causal, log_1m_beta, 0.0)
        row_sum = jnp.broadcast_to(jnp.sum(log_1m_beta, axis=-1, keepdims=True),
                                   (B_UNIT_ROWS, HEAD_PAIR_WIDTH))
        if diagonal:
            tail_ref[rows, :] = row_sum
            logit_refs[u % 2][...] = log_beta
        else:
            tail = tail_ref[rows, :]
            tail_ref[rows, :] = tail + row_sum
            logit_refs[u % 2][...] = log_beta + jnp.concatenate(
                [tail] * (B_KEY_BLOCK // HEAD_PAIR_WIDTH), axis=1)
        addend_refs[u % 2][...] = log_1m_beta.astype(_BF16)

    def weights(u):
        later = jnp.dot(addend_refs[u % 2][...], tri_ref[...], preferred_element_type=_F32)
        a_refs[u % 2][...] = jnp.exp(logit_refs[u % 2][...] + later).astype(_BF16)

    def values(u):
        g, kj = units[u]
        rows = slice(g * B_UNIT_ROWS, (g + 1) * B_UNIT_ROWS)
        keys = slice(kj * B_KEY_BLOCK, (kj + 1) * B_KEY_BLOCK)
        out = jnp.dot(a_refs[u % 2][...], v_ref[keys, :], preferred_element_type=_F32)
        if g == kj:
            acc_ref[rows, :] = out
        else:
            acc_ref[rows, :] += out

    n_units = len(units)
    for t in range(-2, n_units):
        if t + 2 < n_units:
            scores(t + 2)
        if 0 <= t + 1 < n_units:
            weights(t + 1)
        if t >= 0:
            values(t)

    for qi in range(seq // B_Q_BLOCK):
        r0 = qi * HEADS_PER_STEP * B_Q_BLOCK
        o_ref[qi * B_Q_BLOCK:(qi + 1) * B_Q_BLOCK, :] = jnp.where(
            first_head, acc_ref[r0:r0 + B_Q_BLOCK, :], acc_ref[r0 + B_Q_BLOCK:r0 + 2 * B_Q_BLOCK, :]
        ).astype(o_ref.dtype)


def _attn_b(qkv3d):
    batch, seq, _ = qkv3d.shape
    in_specs, out_spec = _head_pair_specs(seq)
    stacked = HEADS_PER_STEP * seq
    return pl.pallas_call(
        _attn_b_kernel,
        grid=(batch, N_HEAD_PAIRS),
        in_specs=in_specs,
        out_specs=out_spec,
        out_shape=jax.ShapeDtypeStruct((batch, seq, D_MODEL), _BF16),
        scratch_shapes=[
            pltpu.VMEM((stacked, HEAD_PAIR_WIDTH), _BF16),
            pltpu.VMEM((stacked, HEAD_PAIR_WIDTH), _F32),
            pltpu.VMEM((stacked, HEAD_PAIR_WIDTH), _F32),
            pltpu.VMEM((B_KEY_BLOCK, B_KEY_BLOCK), _BF16),
            pltpu.VMEM((B_UNIT_ROWS, B_KEY_BLOCK), _F32),
            pltpu.VMEM((B_UNIT_ROWS, B_KEY_BLOCK), _F32),
            pltpu.VMEM((B_UNIT_ROWS, B_KEY_BLOCK), _BF16),
            pltpu.VMEM((B_UNIT_ROWS, B_KEY_BLOCK), _BF16),
            pltpu.VMEM((B_UNIT_ROWS, B_KEY_BLOCK), _BF16),
            pltpu.VMEM((B_UNIT_ROWS, B_KEY_BLOCK), _BF16),
        ],
        compiler_params=_compiler_params(2),
        name="attn_stick",
    )(qkv3d, qkv3d, qkv3d)


def _scaled_qkv_weights(w_qkv):
    scale = HEAD_DIM ** -0.5
    col_scale = jnp.concatenate([jnp.full((D_MODEL,), scale, _F32), jnp.ones((2 * D_MODEL,), _F32)])
    return (w_qkv * col_scale).astype(_BF16)


def kernel(x, w_qkv_a, w_o_a, rel_bias, w_qkv_b, w_o_b, ffn_w_gate, ffn_w_up, ffn_w_down, ln_g, ln_b):
    batch, seq, d = x.shape
    assert d == D_MODEL and (batch * seq) % ROW_TILE == 0
    assert seq % B_KEY_BLOCK == 0 and seq % A_Q_BLOCK == 0
    assert HEAD_DIM ** -0.5 == 0.125

    wg = ffn_w_gate.astype(_BF16)
    wu = ffn_w_up.astype(_BF16)
    wd = ffn_w_down.astype(_BF16)
    wqkv_a = _scaled_qkv_weights(w_qkv_a)
    wqkv_b = _scaled_qkv_weights(w_qkv_b)
    wo_a = w_o_a.astype(_BF16)
    wo_b = w_o_b.astype(_BF16)
    band_bias = _band_bias(rel_bias)
    gains = ln_g.reshape(DEPTH, 3, 1, D_MODEL)
    biases = ln_b.reshape(DEPTH, 3, 1, D_MODEL)

    x2d = x.reshape(batch * seq, d)
    for i in range(DEPTH):
        x2d = _ffn_ln(x2d, wg[i, 0], wu[i, 0], wd[i, 0], gains[i, 0], biases[i, 0])
        j = i // 2
        if i % 2 == 0:
            qkv = _qkv_proj(x2d, wqkv_a[j]).reshape(batch, seq, 3 * d)
            attn = _attn_a(qkv, band_bias)
            w_o = wo_a[j]
        else:
            qkv = _qkv_proj(x2d, wqkv_b[j]).reshape(batch, seq, 3 * d)
            attn = _attn_b(qkv)
            w_o = wo_b[j]
        x2d = _proj_ln(attn.reshape(batch * seq, d), x2d, w_o, gains[i, 1], biases[i, 1])
        x2d = _ffn_ln(x2d, wg[i, 1], wu[i, 1], wd[i, 1], gains[i, 2], biases[i, 2])
    return x2d.reshape(batch, seq, d)
```

```python
import numpy as np
import jax
import jax.numpy as jnp
from jax import lax
from jax.experimental import pallas as pl
from jax.experimental.pallas import tpu as pltpu

D_MODEL = 1024
DEPTH = 4
N_HEADS = 16
HEAD_DIM = D_MODEL // N_HEADS
HEADS_PER_STEP = 2
HEAD_PAIR_WIDTH = HEADS_PER_STEP * HEAD_DIM
N_HEAD_PAIRS = N_HEADS // HEADS_PER_STEP
D_FF = 2816
CHUNK = 64
LEFT_CHUNKS = 8
LEFT_CONTEXT = LEFT_CHUNKS * CHUNK
REL_CLIP = 128
ALPHA = (2.0 * DEPTH) ** 0.25
LN_EPS = 1e-5

ROW_TILE = 512
A_Q_BLOCK = 128
A_KEY_WINDOW = LEFT_CONTEXT + A_Q_BLOCK
B_Q_BLOCK = 128
B_KEY_BLOCK = 256
B_UNIT_ROWS = HEADS_PER_STEP * B_KEY_BLOCK
VMEM_LIMIT_BYTES = 56 * 1024 * 1024

_BF16 = jnp.bfloat16
_F32 = jnp.float32


def _compiler_params(n_grid_dims):
    return pltpu.CompilerParams(
        dimension_semantics=("arbitrary",) * n_grid_dims,
        vmem_limit_bytes=VMEM_LIMIT_BYTES,
    )


def _layer_norm_rows(y, gain, bias):
    mu = jnp.mean(y, axis=-1, keepdims=True)
    d = y - mu
    var = jnp.mean(d * d, axis=-1, keepdims=True)
    return d * lax.rsqrt(var + LN_EPS) * gain + bias


def _stacked(array, lead):
    tail = array.shape[len(lead):]
    index = tuple(lead) + (0,) * len(tail)
    return pl.BlockSpec((None,) * len(lead) + tail, lambda *_: index, pipeline_mode=pl.Buffered(1))


def _row_spec(width):
    return pl.BlockSpec((ROW_TILE, width), lambda i: (i, 0))


def _swiglu_ln(x, wg_ref, wu_ref, wd_ref, gain_ref, bias_ref):
    xb = x.astype(_BF16)
    gate = jnp.dot(xb, wg_ref[...], preferred_element_type=_F32)
    up = jnp.dot(xb, wu_ref[...], preferred_element_type=_F32)
    hidden = (gate * jax.nn.sigmoid(gate) * up).astype(_BF16)
    h = jnp.dot(hidden, wd_ref[...], preferred_element_type=_F32)
    return _layer_norm_rows(ALPHA * x + 0.5 * h, gain_ref[...], bias_ref[...])


def _ffn_ln_qkv_kernel(x_ref, wg_ref, wu_ref, wd_ref, gain_ref, bias_ref, wqkv_ref, x_out_ref, qkv_ref):
    x1 = _swiglu_ln(x_ref[...], wg_ref, wu_ref, wd_ref, gain_ref, bias_ref)
    x_out_ref[...] = x1
    qkv_ref[...] = jnp.dot(x1.astype(_BF16), wqkv_ref[...],
                           preferred_element_type=_F32).astype(qkv_ref.dtype)


def _ffn_ln_qkv(x2d, ffn_weights, ln_params, layer, w_qkv, mixer_layer):
    n_rows, d = x2d.shape
    n_out = w_qkv.shape[-1]
    wg, wu, wd = ffn_weights
    gains, biases = ln_params
    return pl.pallas_call(
        _ffn_ln_qkv_kernel,
        grid=(n_rows // ROW_TILE,),
        in_specs=[_row_spec(d)] + [_stacked(w, (layer, 0)) for w in (wg, wu, wd)]
                 + [_stacked(p, (layer, 0)) for p in (gains, biases)]
                 + [_stacked(w_qkv, (mixer_layer,))],
        out_specs=[_row_spec(d), _row_spec(n_out)],
        out_shape=[jax.ShapeDtypeStruct((n_rows, d), _F32),
                   jax.ShapeDtypeStruct((n_rows, n_out), _BF16)],
        compiler_params=_compiler_params(1),
        name="ffn_ln_qkv",
    )(x2d, wg, wu, wd, gains, biases, w_qkv)


def _proj_ln_ffn_ln_kernel(a_ref, x_ref, wo_ref, gain1_ref, bias1_ref,
                           wg_ref, wu_ref, wd_ref, gain2_ref, bias2_ref, o_ref):
    y = jnp.dot(a_ref[...], wo_ref[...], preferred_element_type=_F32)
    x1 = _layer_norm_rows(ALPHA * x_ref[...] + y, gain1_ref[...], bias1_ref[...])
    o_ref[...] = _swiglu_ln(x1, wg_ref, wu_ref, wd_ref, gain2_ref, bias2_ref)


def _proj_ln_ffn_ln(attn2d, x2d, w_o, mixer_layer, ffn_weights, ln_params, layer):
    n_rows, d = x2d.shape
    wg, wu, wd = ffn_weights
    gains, biases = ln_params
    return pl.pallas_call(
        _proj_ln_ffn_ln_kernel,
        grid=(n_rows // ROW_TILE,),
        in_specs=[_row_spec(d), _row_spec(d), _stacked(w_o, (mixer_layer,))]
                 + [_stacked(p, (layer, 1)) for p in (gains, biases)]
                 + [_stacked(w, (layer, 1)) for w in (wg, wu, wd)]
                 + [_stacked(p, (layer, 2)) for p in (gains, biases)],
        out_specs=_row_spec(d),
        out_shape=jax.ShapeDtypeStruct((n_rows, d), _F32),
        compiler_params=_compiler_params(1),
        name="proj_ln_ffn_ln",
    )(attn2d, x2d, w_o, gains, biases, wg, wu, wd, gains, biases)


def _head_lane_masks(dtype):
    lane = lax.broadcasted_iota(jnp.int32, (1, HEAD_PAIR_WIDTH), 1)
    first = lane < HEAD_DIM
    return [first.astype(dtype), (~first).astype(dtype)], first


def _head_pair_specs(seq):
    def spec(offset):
        return pl.BlockSpec((None, seq, HEAD_PAIR_WIDTH), lambda b, hp: (b, 0, offset + hp))
    return [spec(0), spec(N_HEAD_PAIRS), spec(2 * N_HEAD_PAIRS)], spec(0)


def _attn_a_kernel(q_ref, k_ref, v_ref, diag_ref, o_ref, qs_ref, bias_ref,
                   p0_ref, p1_ref, denom0_ref, denom1_ref):
    seq = q_ref.shape[0]
    head_masks, first_head = _head_lane_masks(_BF16)
    p_refs, denom_refs = (p0_ref, p1_ref), (denom0_ref, denom1_ref)
    stacked_rows = HEADS_PER_STEP * A_Q_BLOCK

    for qi in range(seq // A_Q_BLOCK):
        q2 = q_ref[qi * A_Q_BLOCK:(qi + 1) * A_Q_BLOCK, :]
        for h in range(HEADS_PER_STEP):
            r0 = (qi * HEADS_PER_STEP + h) * A_Q_BLOCK
            qs_ref[r0:r0 + A_Q_BLOCK, :] = q2 * head_masks[h]

    i = lax.broadcasted_iota(jnp.int32, (A_Q_BLOCK, A_KEY_WINDOW), 0)
    j = lax.broadcasted_iota(jnp.int32, (A_Q_BLOCK, A_KEY_WINDOW), 1)
    in_band = (j // CHUNK >= i // CHUNK) & (j // CHUNK <= i // CHUNK + LEFT_CHUNKS)
    diag_len = diag_ref.shape[-1]
    for h in range(HEADS_PER_STEP):
        rows = jnp.broadcast_to(diag_ref[h], (A_Q_BLOCK, diag_len))
        toeplitz = pltpu.roll(rows, diag_len - A_Q_BLOCK, 1, stride=1, stride_axis=0)
        bias_ref[h * A_Q_BLOCK:(h + 1) * A_Q_BLOCK, :] = jnp.where(
            in_band, toeplitz[:, :A_KEY_WINDOW], -jnp.inf)

    def window(qi):
        stop = (qi + 1) * A_Q_BLOCK
        start = max(stop - A_KEY_WINDOW, 0)
        return slice(start, stop), stop - start

    def numerators(qi):
        keys, n_keys = window(qi)
        s = lax.dot_general(qs_ref[qi * stacked_rows:(qi + 1) * stacked_rows, :], k_ref[keys, :],
                            (((1,), (1,)), ((), ())), preferred_element_type=_F32)
        s = s + bias_ref[:, A_KEY_WINDOW - n_keys:]
        p = jnp.exp(s - jnp.max(s, axis=-1, keepdims=True))
        denom_refs[qi % 2][...] = jnp.broadcast_to(jnp.sum(p, axis=-1, keepdims=True),
                                                   (stacked_rows, HEAD_PAIR_WIDTH))
        p_refs[qi % 2][:, :n_keys] = p.astype(_BF16)

    def values(qi):
        keys, n_keys = window(qi)
        o = jnp.dot(p_refs[qi % 2][:, :n_keys], v_ref[keys, :], preferred_element_type=_F32)
        o = o / denom_refs[qi % 2][...]
        o_ref[qi * A_Q_BLOCK:(qi + 1) * A_Q_BLOCK, :] = jnp.where(
            first_head, o[:A_Q_BLOCK], o[A_Q_BLOCK:]).astype(o_ref.dtype)

    n_blocks = seq // A_Q_BLOCK
    for t in range(-1, n_blocks):
        if t + 1 < n_blocks:
            numerators(t + 1)
        if t >= 0:
            values(t)


def _bias_diagonals(rel_table):
    width = A_KEY_WINDOW + A_Q_BLOCK
    far = jnp.broadcast_to(rel_table[2 * REL_CLIP], (width - 2 * REL_CLIP, rel_table.shape[1]))
    near = rel_table[2 * REL_CLIP:0:-1]
    return jnp.concatenate([far, near], axis=0).T.reshape(rel_table.shape[1], 1, width).astype(_F32)


def _attn_a(qkv3d, bias_diagonals):
    batch, seq, _ = qkv3d.shape
    in_specs, out_spec = _head_pair_specs(seq)
    width = bias_diagonals.shape[-1]
    diag_spec = pl.BlockSpec((HEADS_PER_STEP, 1, width), lambda b, hp: (hp, 0, 0))
    stacked_rows = HEADS_PER_STEP * A_Q_BLOCK
    return pl.pallas_call(
        _attn_a_kernel,
        grid=(batch, N_HEAD_PAIRS),
        in_specs=in_specs + [diag_spec],
        out_specs=out_spec,
        out_shape=jax.ShapeDtypeStruct((batch, seq, D_MODEL), _BF16),
        scratch_shapes=[
            pltpu.VMEM((HEADS_PER_STEP * seq, HEAD_PAIR_WIDTH), _BF16),
            pltpu.VMEM((stacked_rows, A_KEY_WINDOW), _F32),
            pltpu.VMEM((stacked_rows, A_KEY_WINDOW), _BF16),
            pltpu.VMEM((stacked_rows, A_KEY_WINDOW), _BF16),
            pltpu.VMEM((stacked_rows, HEAD_PAIR_WIDTH), _F32),
            pltpu.VMEM((stacked_rows, HEAD_PAIR_WIDTH), _F32),
        ],
        compiler_params=_compiler_params(2),
        name="attn_band",
    )(qkv3d, qkv3d, qkv3d, bias_diagonals)


def _stick_units(seq):
    n = seq // B_KEY_BLOCK
    return [(g, kj) for kj in range(n - 1, -1, -1) for g in range(kj, n)]


def _attn_b_kernel(q_ref, k_ref, v_ref, o_ref,
                   qs_ref, acc_ref, tail_ref, tri_ref,
                   logit0_ref, logit1_ref, addend0_ref, addend1_ref, a0_ref, a1_ref):
    seq = q_ref.shape[0]
    head_masks, first_head = _head_lane_masks(_BF16)
    logit_refs, addend_refs, a_refs = (logit0_ref, logit1_ref), (addend0_ref, addend1_ref), (a0_ref, a1_ref)

    for qi in range(seq // B_Q_BLOCK):
        q2 = q_ref[qi * B_Q_BLOCK:(qi + 1) * B_Q_BLOCK, :]
        for h in range(HEADS_PER_STEP):
            r0 = (qi * HEADS_PER_STEP + h) * B_Q_BLOCK
            qs_ref[r0:r0 + B_Q_BLOCK, :] = q2 * head_masks[h]

    key_j = lax.broadcasted_iota(jnp.int32, (B_KEY_BLOCK, B_KEY_BLOCK), 0)
    key_s = lax.broadcasted_iota(jnp.int32, (B_KEY_BLOCK, B_KEY_BLOCK), 1)
    tri_ref[...] = (key_j > key_s).astype(_BF16)

    row = lax.broadcasted_iota(jnp.int32, (B_UNIT_ROWS, B_KEY_BLOCK), 0)
    col = lax.broadcasted_iota(jnp.int32, (B_UNIT_ROWS, B_KEY_BLOCK), 1)
    q_in_block = (row // (HEADS_PER_STEP * B_Q_BLOCK)) * B_Q_BLOCK + row % B_Q_BLOCK
    causal = col < q_in_block
    sign_bit = jnp.uint32(0x80000000)

    units = _stick_units(seq)

    def scores(u):
        g, kj = units[u]
        diagonal = g == kj
        rows = slice(g * B_UNIT_ROWS, (g + 1) * B_UNIT_ROWS)
        keys = slice(kj * B_KEY_BLOCK, (kj + 1) * B_KEY_BLOCK)
        z = lax.dot_general(qs_ref[rows, :], k_ref[keys, :], (((1,), (1,)), ((), ())),
                            preferred_element_type=_F32)
        neg_abs = lax.bitcast_convert_type(lax.bitcast_convert_type(z, jnp.uint32) | sign_bit, _F32)
        log_beta = jnp.minimum(z, 0.0) - jnp.log(1.0 + jnp.exp(neg_abs))
        log_1m_beta = log_beta - z
        if diagonal:
            log_beta = jnp.where(causal, log_beta, -jnp.inf)
            log_1m_beta = jnp.where(causal, log_1m_beta, 0.0)
        row_sum = jnp.broadcast_to(jnp.sum(log_1m_beta, axis=-1, keepdims=True),
                                   (B_UNIT_ROWS, HEAD_PAIR_WIDTH))
        if diagonal:
            tail_ref[rows, :] = row_sum
            logit_refs[u % 2][...] = log_beta
        else:
            tail = tail_ref[rows, :]
            tail_ref[rows, :] = tail + row_sum
            logit_refs[u % 2][...] = log_beta + jnp.concatenate(
                [tail] * (B_KEY_BLOCK // HEAD_PAIR_WIDTH), axis=1)
        addend_refs[u % 2][...] = log_1m_beta.astype(_BF16)

    def weights(u):
        later = jnp.dot(addend_refs[u % 2][...], tri_ref[...], preferred_element_type=_F32)
        a_refs[u % 2][...] = jnp.exp(logit_refs[u % 2][...] + later).astype(_BF16)

    def values(u):
        g, kj = units[u]
        rows = slice(g * B_UNIT_ROWS, (g + 1) * B_UNIT_ROWS)
        keys = slice(kj * B_KEY_BLOCK, (kj + 1) * B_KEY_BLOCK)
        out = jnp.dot(a_refs[u % 2][...], v_ref[keys, :], preferred_element_type=_F32)
        if g == kj:
            acc_ref[rows, :] = out
        else:
            acc_ref[rows, :] += out

    n_units = len(units)
    for t in range(-2, n_units):
        if t + 2 < n_units:
            scores(t + 2)
        if 0 <= t + 1 < n_units:
            weights(t + 1)
        if t >= 0:
            values(t)

    for qi in range(seq // B_Q_BLOCK):
        r0 = qi * HEADS_PER_STEP * B_Q_BLOCK
        o_ref[qi * B_Q_BLOCK:(qi + 1) * B_Q_BLOCK, :] = jnp.where(
            first_head, acc_ref[r0:r0 + B_Q_BLOCK, :], acc_ref[r0 + B_Q_BLOCK:r0 + 2 * B_Q_BLOCK, :]
        ).astype(o_ref.dtype)


def _attn_b(qkv3d):
    batch, seq, _ = qkv3d.shape
    in_specs, out_spec = _head_pair_specs(seq)
    stacked = HEADS_PER_STEP * seq
    return pl.pallas_call(
        _attn_b_kernel,
        grid=(batch, N_HEAD_PAIRS),
        in_specs=in_specs,
        out_specs=out_spec,
        out_shape=jax.ShapeDtypeStruct((batch, seq, D_MODEL), _BF16),
        scratch_shapes=[
            pltpu.VMEM((stacked, HEAD_PAIR_WIDTH), _BF16),
            pltpu.VMEM((stacked, HEAD_PAIR_WIDTH), _F32),
            pltpu.VMEM((stacked, HEAD_PAIR_WIDTH), _F32),
            pltpu.VMEM((B_KEY_BLOCK, B_KEY_BLOCK), _BF16),
            pltpu.VMEM((B_UNIT_ROWS, B_KEY_BLOCK), _F32),
            pltpu.VMEM((B_UNIT_ROWS, B_KEY_BLOCK), _F32),
            pltpu.VMEM((B_UNIT_ROWS, B_KEY_BLOCK), _BF16),
            pltpu.VMEM((B_UNIT_ROWS, B_KEY_BLOCK), _BF16),
            pltpu.VMEM((B_UNIT_ROWS, B_KEY_BLOCK), _BF16),
            pltpu.VMEM((B_UNIT_ROWS, B_KEY_BLOCK), _BF16),
        ],
        compiler_params=_compiler_params(2),
        name="attn_stick",
    )(qkv3d, qkv3d, qkv3d)


def _scaled_qkv_weights(w_qkv):
    scale = HEAD_DIM ** -0.5
    col_scale = jnp.concatenate([jnp.full((D_MODEL,), scale, _F32), jnp.ones((2 * D_MODEL,), _F32)])
    return (w_qkv * col_scale).astype(_BF16)


def kernel(x, w_qkv_a, w_o_a, rel_bias, w_qkv_b, w_o_b, ffn_w_gate, ffn_w_up, ffn_w_down, ln_g, ln_b):
    batch, seq, d = x.shape
    assert d == D_MODEL and (batch * seq) % ROW_TILE == 0
    assert seq % B_KEY_BLOCK == 0 and seq % A_Q_BLOCK == 0
    assert HEAD_DIM ** -0.5 == 0.125

    ffn_weights = tuple(w.astype(_BF16) for w in (ffn_w_gate, ffn_w_up, ffn_w_down))
    ln_params = (ln_g.reshape(DEPTH, 3, 1, D_MODEL), ln_b.reshape(DEPTH, 3, 1, D_MODEL))
    w_qkv = (_scaled_qkv_weights(w_qkv_a), _scaled_qkv_weights(w_qkv_b))
    w_o = (w_o_a.astype(_BF16), w_o_b.astype(_BF16))
    bias_diagonals = _bias_diagonals(rel_bias)

    x2d = x.reshape(batch * seq, d)
    for layer in range(DEPTH):
        mixer, mixer_layer = layer % 2, layer // 2
        x2d, qkv = _ffn_ln_qkv(x2d, ffn_weights, ln_params, layer, w_qkv[mixer], mixer_layer)
        qkv = qkv.reshape(batch, seq, 3 * d)
        attn = _attn_a(qkv, bias_diagonals) if mixer == 0 else _attn_b(qkv)
        x2d = _proj_ln_ffn_ln(attn.reshape(batch * seq, d), x2d, w_o[mixer], mixer_layer,
                              ffn_weights, ln_params, layer)
    return x2d.reshape(batch, seq, d)
```

```python
import functools
import math

import jax
import jax.numpy as jnp
from jax import lax
from jax.experimental import pallas as pl
from jax.experimental.pallas import tpu as pltpu

D_MODEL = 1024
DEPTH = 4
N_HEADS = 16
HEAD_DIM = D_MODEL // N_HEADS
HEADS_PER_STEP = 2
HEAD_PAIR_WIDTH = HEADS_PER_STEP * HEAD_DIM
N_HEAD_PAIRS = N_HEADS // HEADS_PER_STEP
D_FF = 2816
CHUNK = 64
LEFT_CHUNKS = 8
LEFT_CONTEXT = LEFT_CHUNKS * CHUNK
REL_CLIP = 128
ALPHA = (2.0 * DEPTH) ** 0.25
LN_EPS = 1e-5
LOG2_E = math.log2(math.e)

ROW_TILE = 512
A_Q_BLOCK = 128
A_KEY_WINDOW = LEFT_CONTEXT + A_Q_BLOCK
B_Q_BLOCK = 128
B_KEY_BLOCK = 256
B_UNIT_ROWS = HEADS_PER_STEP * B_KEY_BLOCK
VMEM_LIMIT_BYTES = 56 * 1024 * 1024

_BF16 = jnp.bfloat16
_F32 = jnp.float32


def _compiler_params(n_grid_dims):
    return pltpu.CompilerParams(
        dimension_semantics=("arbitrary",) * n_grid_dims,
        vmem_limit_bytes=VMEM_LIMIT_BYTES,
    )


def _layer_norm_rows(y, gain, bias):
    mu = jnp.mean(y, axis=-1, keepdims=True)
    d = y - mu
    var = jnp.mean(d * d, axis=-1, keepdims=True)
    return d * lax.rsqrt(var + LN_EPS) * gain + bias


def _stacked(array, lead):
    tail = array.shape[len(lead):]
    index = tuple(lead) + (0,) * len(tail)
    return pl.BlockSpec((None,) * len(lead) + tail, lambda *_: index, pipeline_mode=pl.Buffered(1))


def _row_spec(width):
    return pl.BlockSpec((ROW_TILE, width), lambda i: (i, 0))


def _swiglu_ln(x, wg_ref, wu_ref, wd_ref, gain_ref, bias_ref):
    xb = x.astype(_BF16)
    gate = jnp.dot(xb, wg_ref[...], preferred_element_type=_F32)
    up = jnp.dot(xb, wu_ref[...], preferred_element_type=_F32)
    hidden = (gate * jax.nn.sigmoid(gate) * up).astype(_BF16)
    h = jnp.dot(hidden, wd_ref[...], preferred_element_type=_F32)
    return _layer_norm_rows(ALPHA * x + 0.5 * h, gain_ref[...], bias_ref[...])


def _ffn_ln_qkv_kernel(x_ref, wg_ref, wu_ref, wd_ref, gain_ref, bias_ref, wqkv_ref, x_out_ref, qkv_ref,
                       *, q_scale):
    x1 = _swiglu_ln(x_ref[...], wg_ref, wu_ref, wd_ref, gain_ref, bias_ref)
    x_out_ref[...] = x1
    qkv = jnp.dot(x1.astype(_BF16), wqkv_ref[...], preferred_element_type=_F32)
    d = x1.shape[-1]
    qkv_ref[:, :d] = (qkv[:, :d] * q_scale).astype(qkv_ref.dtype)
    qkv_ref[:, d:] = qkv[:, d:].astype(qkv_ref.dtype)


def _ffn_ln_qkv(x2d, ffn_weights, ln_params, layer, w_qkv, mixer_layer, q_scale):
    n_rows, d = x2d.shape
    n_out = w_qkv.shape[-1]
    wg, wu, wd = ffn_weights
    gains, biases = ln_params
    return pl.pallas_call(
        functools.partial(_ffn_ln_qkv_kernel, q_scale=q_scale),
        grid=(n_rows // ROW_TILE,),
        in_specs=[_row_spec(d)] + [_stacked(w, (layer, 0)) for w in (wg, wu, wd)]
                 + [_stacked(p, (layer, 0)) for p in (gains, biases)]
                 + [_stacked(w_qkv, (mixer_layer,))],
        out_specs=[_row_spec(d), _row_spec(n_out)],
        out_shape=[jax.ShapeDtypeStruct((n_rows, d), _F32),
                   jax.ShapeDtypeStruct((n_rows, n_out), _BF16)],
        compiler_params=_compiler_params(1),
        name="ffn_ln_qkv",
    )(x2d, wg, wu, wd, gains, biases, w_qkv)


def _proj_ln_ffn_ln_kernel(a_ref, x_ref, wo_ref, gain1_ref, bias1_ref,
                           wg_ref, wu_ref, wd_ref, gain2_ref, bias2_ref, o_ref):
    y = jnp.dot(a_ref[...], wo_ref[...], preferred_element_type=_F32)
    x1 = _layer_norm_rows(ALPHA * x_ref[...] + y, gain1_ref[...], bias1_ref[...])
    o_ref[...] = _swiglu_ln(x1, wg_ref, wu_ref, wd_ref, gain2_ref, bias2_ref)


def _proj_ln_ffn_ln(attn2d, x2d, w_o, mixer_layer, ffn_weights, ln_params, layer):
    n_rows, d = x2d.shape
    wg, wu, wd = ffn_weights
    gains, biases = ln_params
    return pl.pallas_call(
        _proj_ln_ffn_ln_kernel,
        grid=(n_rows // ROW_TILE,),
        in_specs=[_row_spec(d), _row_spec(d), _stacked(w_o, (mixer_layer,))]
                 + [_stacked(p, (layer, 1)) for p in (gains, biases)]
                 + [_stacked(w, (layer, 1)) for w in (wg, wu, wd)]
                 + [_stacked(p, (layer, 2)) for p in (gains, biases)],
        out_specs=_row_spec(d),
        out_shape=jax.ShapeDtypeStruct((n_rows, d), _F32),
        compiler_params=_compiler_params(1),
        name="proj_ln_ffn_ln",
    )(attn2d, x2d, w_o, gains, biases, wg, wu, wd, gains, biases)


def _head_lane_masks(dtype):
    lane = lax.broadcasted_iota(jnp.int32, (1, HEAD_PAIR_WIDTH), 1)
    first = lane < HEAD_DIM
    return [first.astype(dtype), (~first).astype(dtype)], first


def _head_pair_specs(seq):
    def spec(offset):
        return pl.BlockSpec((None, seq, HEAD_PAIR_WIDTH), lambda b, hp: (b, 0, offset + hp))
    return [spec(0), spec(N_HEAD_PAIRS), spec(2 * N_HEAD_PAIRS)], spec(0)


def _band_block_order(n_blocks):
    n_short = min(LEFT_CONTEXT // A_Q_BLOCK, n_blocks)
    short, full = list(range(n_short)), list(range(n_short, n_blocks))
    order = []
    while full or short:
        order += full[:2]
        full = full[2:]
        order += short[:1]
        short = short[1:]
    return order


def _attn_a_kernel(q_ref, k_ref, v_ref, diag_ref, o_ref, qs_ref, bias_ref,
                   s0_ref, s1_ref, max0_ref, max1_ref, p0_ref, p1_ref, denom0_ref, denom1_ref):
    seq = q_ref.shape[0]
    head_masks, first_head = _head_lane_masks(_BF16)
    s_refs, max_refs = (s0_ref, s1_ref), (max0_ref, max1_ref)
    p_refs, denom_refs = (p0_ref, p1_ref), (denom0_ref, denom1_ref)
    stacked_rows = HEADS_PER_STEP * A_Q_BLOCK

    for qi in range(seq // A_Q_BLOCK):
        q2 = q_ref[qi * A_Q_BLOCK:(qi + 1) * A_Q_BLOCK, :]
        for h in range(HEADS_PER_STEP):
            r0 = (qi * HEADS_PER_STEP + h) * A_Q_BLOCK
            qs_ref[r0:r0 + A_Q_BLOCK, :] = q2 * head_masks[h]

    i = lax.broadcasted_iota(jnp.int32, (A_Q_BLOCK, A_KEY_WINDOW), 0)
    j = lax.broadcasted_iota(jnp.int32, (A_Q_BLOCK, A_KEY_WINDOW), 1)
    in_band = (j // CHUNK >= i // CHUNK) & (j // CHUNK <= i // CHUNK + LEFT_CHUNKS)
    diag_len = diag_ref.shape[-1]
    for h in range(HEADS_PER_STEP):
        rows = jnp.broadcast_to(diag_ref[h], (A_Q_BLOCK, diag_len))
        toeplitz = pltpu.roll(rows, diag_len - A_Q_BLOCK, 1, stride=1, stride_axis=0)
        bias_ref[h * A_Q_BLOCK:(h + 1) * A_Q_BLOCK, :] = jnp.where(
            in_band, toeplitz[:, :A_KEY_WINDOW], -jnp.inf)

    order = _band_block_order(seq // A_Q_BLOCK)

    def window(qi):
        stop = (qi + 1) * A_Q_BLOCK
        start = max(stop - A_KEY_WINDOW, 0)
        return slice(start, stop), stop - start

    def lane_tiled(per_row, n_keys):
        return jnp.concatenate([per_row] * (n_keys // HEAD_PAIR_WIDTH), axis=1)

    def scores(u):
        qi = order[u]
        keys, n_keys = window(qi)
        s = lax.dot_general(qs_ref[qi * stacked_rows:(qi + 1) * stacked_rows, :], k_ref[keys, :],
                            (((1,), (1,)), ((), ())), preferred_element_type=_F32)
        s = s + bias_ref[:, A_KEY_WINDOW - n_keys:]
        s_refs[u % 2][:, :n_keys] = s
        max_refs[u % 2][...] = jnp.broadcast_to(jnp.max(s, axis=-1, keepdims=True),
                                                (stacked_rows, HEAD_PAIR_WIDTH))

    def numerators(u):
        _, n_keys = window(order[u])
        p = jnp.exp(s_refs[u % 2][:, :n_keys] - lane_tiled(max_refs[u % 2][...], n_keys))
        denom_refs[u % 2][...] = jnp.broadcast_to(jnp.sum(p, axis=-1, keepdims=True),
                                                  (stacked_rows, HEAD_PAIR_WIDTH))
        p_refs[u % 2][:, :n_keys] = p.astype(_BF16)

    def values(u):
        qi = order[u]
        keys, n_keys = window(qi)
        o = jnp.dot(p_refs[u % 2][:, :n_keys], v_ref[keys, :], preferred_element_type=_F32)
        o = o / denom_refs[u % 2][...]
        o_ref[qi * A_Q_BLOCK:(qi + 1) * A_Q_BLOCK, :] = jnp.where(
            first_head, o[:A_Q_BLOCK], o[A_Q_BLOCK:]).astype(o_ref.dtype)

    n_units = len(order)
    for t in range(-2, n_units):
        if t + 2 < n_units:
            scores(t + 2)
        if 0 <= t + 1 < n_units:
            numerators(t + 1)
        if t >= 0:
            values(t)


def _bias_diagonals(rel_table):
    width = A_KEY_WINDOW + A_Q_BLOCK
    far = jnp.broadcast_to(rel_table[2 * REL_CLIP], (width - 2 * REL_CLIP, rel_table.shape[1]))
    near = rel_table[2 * REL_CLIP:0:-1]
    return jnp.concatenate([far, near], axis=0).T.reshape(rel_table.shape[1], 1, width).astype(_F32)


def _attn_a(qkv3d, bias_diagonals):
    batch, seq, _ = qkv3d.shape
    in_specs, out_spec = _head_pair_specs(seq)
    width = bias_diagonals.shape[-1]
    diag_spec = pl.BlockSpec((HEADS_PER_STEP, 1, width), lambda b, hp: (hp, 0, 0))
    stacked_rows = HEADS_PER_STEP * A_Q_BLOCK
    return pl.pallas_call(
        _attn_a_kernel,
        grid=(batch, N_HEAD_PAIRS),
        in_specs=in_specs + [diag_spec],
        out_specs=out_spec,
        out_shape=jax.ShapeDtypeStruct((batch, seq, D_MODEL), _BF16),
        scratch_shapes=[
            pltpu.VMEM((HEADS_PER_STEP * seq, HEAD_PAIR_WIDTH), _BF16),
            pltpu.VMEM((stacked_rows, A_KEY_WINDOW), _F32),
            pltpu.VMEM((stacked_rows, A_KEY_WINDOW), _F32),
            pltpu.VMEM((stacked_rows, A_KEY_WINDOW), _F32),
            pltpu.VMEM((stacked_rows, HEAD_PAIR_WIDTH), _F32),
            pltpu.VMEM((stacked_rows, HEAD_PAIR_WIDTH), _F32),
            pltpu.VMEM((stacked_rows, A_KEY_WINDOW), _BF16),
            pltpu.VMEM((stacked_rows, A_KEY_WINDOW), _BF16),
            pltpu.VMEM((stacked_rows, HEAD_PAIR_WIDTH), _F32),
            pltpu.VMEM((stacked_rows, HEAD_PAIR_WIDTH), _F32),
        ],
        compiler_params=_compiler_params(2),
        name="attn_band",
    )(qkv3d, qkv3d, qkv3d, bias_diagonals)


def _stick_units(seq):
    n = seq // B_KEY_BLOCK
    return [(g, kj) for kj in range(n - 1, -1, -1) for g in range(kj, n)]


def _attn_b_kernel(q_ref, k_ref, v_ref, o_ref,
                   qs_ref, acc_ref, tail_ref, tri_ref,
                   drop_ref, keep_ref,
                   logit0_ref, logit1_ref, addend0_ref, addend1_ref, a0_ref, a1_ref):
    seq = q_ref.shape[0]
    head_masks, first_head = _head_lane_masks(_BF16)
    logit_refs, addend_refs, a_refs = (logit0_ref, logit1_ref), (addend0_ref, addend1_ref), (a0_ref, a1_ref)

    for qi in range(seq // B_Q_BLOCK):
        q2 = q_ref[qi * B_Q_BLOCK:(qi + 1) * B_Q_BLOCK, :]
        for h in range(HEADS_PER_STEP):
            r0 = (qi * HEADS_PER_STEP + h) * B_Q_BLOCK
            qs_ref[r0:r0 + B_Q_BLOCK, :] = q2 * head_masks[h]

    key_j = lax.broadcasted_iota(jnp.int32, (B_KEY_BLOCK, B_KEY_BLOCK), 0)
    key_s = lax.broadcasted_iota(jnp.int32, (B_KEY_BLOCK, B_KEY_BLOCK), 1)
    tri_ref[...] = (key_j > key_s).astype(_BF16)

    row = lax.broadcasted_iota(jnp.int32, (B_UNIT_ROWS, B_KEY_BLOCK), 0)
    col = lax.broadcasted_iota(jnp.int32, (B_UNIT_ROWS, B_KEY_BLOCK), 1)
    q_in_block = (row // (HEADS_PER_STEP * B_Q_BLOCK)) * B_Q_BLOCK + row % B_Q_BLOCK
    causal = col < q_in_block
    drop_ref[...] = jnp.where(causal, 0.0, -jnp.inf)
    keep_ref[...] = jnp.where(causal, 1.0, 0.0)
    sign_bit = jnp.uint32(0x80000000)

    units = _stick_units(seq)

    def parts(u):
        g, kj = units[u]
        if g != kj:
            return [(slice(0, B_UNIT_ROWS), B_KEY_BLOCK)], False
        half = B_UNIT_ROWS // 2
        return [(slice(0, half), B_KEY_BLOCK // 2), (slice(half, B_UNIT_ROWS), B_KEY_BLOCK)], True

    def unit_rows(u, part_rows):
        base = units[u][0] * B_UNIT_ROWS
        return slice(base + part_rows.start, base + part_rows.stop)

    def unit_keys(u, n_keys):
        base = units[u][1] * B_KEY_BLOCK
        return slice(base, base + n_keys)

    def scores(u):
        pieces, diagonal = parts(u)
        for part_rows, n_keys in pieces:
            rows = unit_rows(u, part_rows)
            n_rows = part_rows.stop - part_rows.start
            y = lax.dot_general(qs_ref[rows, :], k_ref[unit_keys(u, n_keys), :],
                                (((1,), (1,)), ((), ())), preferred_element_type=_F32)
            neg_abs = lax.bitcast_convert_type(lax.bitcast_convert_type(y, jnp.uint32) | sign_bit, _F32)
            log_beta = jnp.minimum(y, 0.0) - jnp.log(1.0 + jnp.exp2(neg_abs)) * LOG2_E
            log_1m_beta = log_beta - y
            if diagonal:
                log_beta = log_beta + drop_ref[part_rows, :n_keys]
                log_1m_beta = log_1m_beta * keep_ref[part_rows, :n_keys]
            row_sum = jnp.broadcast_to(jnp.sum(log_1m_beta, axis=-1, keepdims=True),
                                       (n_rows, HEAD_PAIR_WIDTH))
            if diagonal:
                tail_ref[rows, :] = row_sum
                logit_refs[u % 2][part_rows, :n_keys] = log_beta
            else:
                tail = tail_ref[rows, :]
                tail_ref[rows, :] = tail + row_sum
                logit_refs[u % 2][part_rows, :n_keys] = log_beta + jnp.concatenate(
                    [tail] * (n_keys // HEAD_PAIR_WIDTH), axis=1)
            addend_refs[u % 2][part_rows, :n_keys] = log_1m_beta.astype(_BF16)

    def weights(u):
        for part_rows, n_keys in parts(u)[0]:
            later = jnp.dot(addend_refs[u % 2][part_rows, :n_keys], tri_ref[:n_keys, :n_keys],
                            preferred_element_type=_F32)
            a_refs[u % 2][part_rows, :n_keys] = jnp.exp2(
                logit_refs[u % 2][part_rows, :n_keys] + later).astype(_BF16)

    def values(u):
        pieces, diagonal = parts(u)
        for part_rows, n_keys in pieces:
            rows = unit_rows(u, part_rows)
            out = jnp.dot(a_refs[u % 2][part_rows, :n_keys], v_ref[unit_keys(u, n_keys), :],
                          preferred_element_type=_F32)
            if diagonal:
                acc_ref[rows, :] = out
            else:
                acc_ref[rows, :] += out

    n_units = len(units)
    for t in range(-2, n_units):
        if t + 2 < n_units:
            scores(t + 2)
        if 0 <= t + 1 < n_units:
            weights(t + 1)
        if t >= 0:
            values(t)

    for qi in range(seq // B_Q_BLOCK):
        r0 = qi * HEADS_PER_STEP * B_Q_BLOCK
        o_ref[qi * B_Q_BLOCK:(qi + 1) * B_Q_BLOCK, :] = jnp.where(
            first_head, acc_ref[r0:r0 + B_Q_BLOCK, :], acc_ref[r0 + B_Q_BLOCK:r0 + 2 * B_Q_BLOCK, :]
        ).astype(o_ref.dtype)


def _attn_b(qkv3d):
    batch, seq, _ = qkv3d.shape
    in_specs, out_spec = _head_pair_specs(seq)
    stacked = HEADS_PER_STEP * seq
    return pl.pallas_call(
        _attn_b_kernel,
        grid=(batch, N_HEAD_PAIRS),
        in_specs=in_specs,
        out_specs=out_spec,
        out_shape=jax.ShapeDtypeStruct((batch, seq, D_MODEL), _BF16),
        scratch_shapes=[
            pltpu.VMEM((stacked, HEAD_PAIR_WIDTH), _BF16),
            pltpu.VMEM((stacked, HEAD_PAIR_WIDTH), _F32),
            pltpu.VMEM((stacked, HEAD_PAIR_WIDTH), _F32),
            pltpu.VMEM((B_KEY_BLOCK, B_KEY_BLOCK), _BF16),
            pltpu.VMEM((B_UNIT_ROWS, B_KEY_BLOCK), _F32),
            pltpu.VMEM((B_UNIT_ROWS, B_KEY_BLOCK), _F32),
            pltpu.VMEM((B_UNIT_ROWS, B_KEY_BLOCK), _F32),
            pltpu.VMEM((B_UNIT_ROWS, B_KEY_BLOCK), _F32),
            pltpu.VMEM((B_UNIT_ROWS, B_KEY_BLOCK), _BF16),
            pltpu.VMEM((B_UNIT_ROWS, B_KEY_BLOCK), _BF16),
            pltpu.VMEM((B_UNIT_ROWS, B_KEY_BLOCK), _BF16),
            pltpu.VMEM((B_UNIT_ROWS, B_KEY_BLOCK), _BF16),
        ],
        compiler_params=_compiler_params(2),
        name="attn_stick",
    )(qkv3d, qkv3d, qkv3d)


def kernel(x, w_qkv_a, w_o_a, rel_bias, w_qkv_b, w_o_b, ffn_w_gate, ffn_w_up, ffn_w_down, ln_g, ln_b):
    batch, seq, d = x.shape
    assert d == D_MODEL and (batch * seq) % ROW_TILE == 0
    assert seq % B_KEY_BLOCK == 0 and seq % A_Q_BLOCK == 0

    ffn_weights = tuple(w.astype(_BF16) for w in (ffn_w_gate, ffn_w_up, ffn_w_down))
    ln_params = (ln_g.reshape(DEPTH, 3, 1, D_MODEL), ln_b.reshape(DEPTH, 3, 1, D_MODEL))
    w_qkv = (w_qkv_a.astype(_BF16), w_qkv_b.astype(_BF16))
    w_o = (w_o_a.astype(_BF16), w_o_b.astype(_BF16))
    bias_diagonals = _bias_diagonals(rel_bias)
    q_scales = (HEAD_DIM ** -0.5, HEAD_DIM ** -0.5 * LOG2_E)

    x2d = x.reshape(batch * seq, d)
    for layer in range(DEPTH):
        mixer, mixer_layer = layer % 2, layer // 2
        x2d, qkv = _ffn_ln_qkv(x2d, ffn_weights, ln_params, layer, w_qkv[mixer], mixer_layer,
                               q_scales[mixer])
        qkv = qkv.reshape(batch, seq, 3 * d)
        attn = _attn_a(qkv, bias_diagonals) if mixer == 0 else _attn_b(qkv)
        x2d = _proj_ln_ffn_ln(attn.reshape(batch * seq, d), x2d, w_o[mixer], mixer_layer,
                              ffn_weights, ln_params, layer)
    return x2d.reshape(batch, seq, d)
```

```python
import functools
import math

import jax
import jax.numpy as jnp
from jax import lax
from jax.experimental import pallas as pl
from jax.experimental.pallas import tpu as pltpu

D_MODEL = 1024
DEPTH = 4
N_HEADS = 16
HEAD_DIM = D_MODEL // N_HEADS
HEADS_PER_STEP = 2
HEAD_PAIR_WIDTH = HEADS_PER_STEP * HEAD_DIM
N_HEAD_PAIRS = N_HEADS // HEADS_PER_STEP
D_FF = 2816
CHUNK = 64
LEFT_CHUNKS = 8
LEFT_CONTEXT = LEFT_CHUNKS * CHUNK
REL_CLIP = 128
ALPHA = (2.0 * DEPTH) ** 0.25
LN_EPS = 1e-5
LOG2_E = math.log2(math.e)

ROW_TILE = 512
ROW_SUBTILES = 2
A_Q_BLOCK = 128
A_KEY_WINDOW = LEFT_CONTEXT + A_Q_BLOCK
B_Q_BLOCK = 128
B_KEY_BLOCK = 256
B_UNIT_ROWS = HEADS_PER_STEP * B_KEY_BLOCK
VMEM_LIMIT_BYTES = 56 * 1024 * 1024

_BF16 = jnp.bfloat16
_F32 = jnp.float32


def _compiler_params(n_grid_dims):
    return pltpu.CompilerParams(
        dimension_semantics=("arbitrary",) * n_grid_dims,
        vmem_limit_bytes=VMEM_LIMIT_BYTES,
    )


def _layer_norm_rows(y, gain, bias):
    mu = jnp.mean(y, axis=-1, keepdims=True)
    d = y - mu
    var = jnp.mean(d * d, axis=-1, keepdims=True)
    return d * lax.rsqrt(var + LN_EPS) * gain + bias


def _stacked(array, lead):
    tail = array.shape[len(lead):]
    index = tuple(lead) + (0,) * len(tail)
    return pl.BlockSpec((None,) * len(lead) + tail, lambda *_: index, pipeline_mode=pl.Buffered(1))


def _row_spec(width):
    return pl.BlockSpec((ROW_TILE, width), lambda i: (i, 0))


def _swiglu(x, wg_ref, wu_ref, wd_ref):
    xb = x.astype(_BF16)
    gate = jnp.dot(xb, wg_ref[...], preferred_element_type=_F32)
    up = jnp.dot(xb, wu_ref[...], preferred_element_type=_F32)
    hidden = (gate * jax.nn.sigmoid(gate) * up).astype(_BF16)
    return jnp.dot(hidden, wd_ref[...], preferred_element_type=_F32)


def _row_subtiles():
    rows = ROW_TILE // ROW_SUBTILES
    return [slice(r * rows, (r + 1) * rows) for r in range(ROW_SUBTILES)]


def _ffn_ln_qkv_kernel(x_ref, wg_ref, wu_ref, wd_ref, gain_ref, bias_ref, wqkv_ref, x_out_ref, qkv_ref,
                       *, q_scale):
    d = x_ref.shape[-1]
    subtiles = _row_subtiles()

    def finish(rows, h):
        x1 = _layer_norm_rows(ALPHA * x_ref[rows, :] + 0.5 * h, gain_ref[...], bias_ref[...])
        x_out_ref[rows, :] = x1
        qkv = jnp.dot(x1.astype(_BF16), wqkv_ref[...], preferred_element_type=_F32)
        qkv_ref[rows, :d] = (qkv[:, :d] * q_scale).astype(qkv_ref.dtype)
        qkv_ref[rows, d:] = qkv[:, d:].astype(qkv_ref.dtype)

    pending = None
    for rows in subtiles:
        h = _swiglu(x_ref[rows, :], wg_ref, wu_ref, wd_ref)
        if pending is not None:
            finish(*pending)
        pending = (rows, h)
    finish(*pending)


def _ffn_ln_qkv(x2d, ffn_weights, ln_params, layer, w_qkv, mixer_layer, q_scale):
    n_rows, d = x2d.shape
    n_out = w_qkv.shape[-1]
    wg, wu, wd = ffn_weights
    gains, biases = ln_params
    return pl.pallas_call(
        functools.partial(_ffn_ln_qkv_kernel, q_scale=q_scale),
        grid=(n_rows // ROW_TILE,),
        in_specs=[_row_spec(d)] + [_stacked(w, (layer, 0)) for w in (wg, wu, wd)]
                 + [_stacked(p, (layer, 0)) for p in (gains, biases)]
                 + [_stacked(w_qkv, (mixer_layer,))],
        out_specs=[_row_spec(d), _row_spec(n_out)],
        out_shape=[jax.ShapeDtypeStruct((n_rows, d), _F32),
                   jax.ShapeDtypeStruct((n_rows, n_out), _BF16)],
        compiler_params=_compiler_params(1),
        name="ffn_ln_qkv",
    )(x2d, wg, wu, wd, gains, biases, w_qkv)


def _proj_ln_ffn_ln_kernel(a_ref, x_ref, wo_ref, gain1_ref, bias1_ref,
                           wg_ref, wu_ref, wd_ref, gain2_ref, bias2_ref, o_ref):
    subtiles = _row_subtiles()
    ys = [jnp.dot(a_ref[rows, :], wo_ref[...], preferred_element_type=_F32) for rows in subtiles]
    pending = None
    for rows, y in zip(subtiles, ys):
        x1 = _layer_norm_rows(ALPHA * x_ref[rows, :] + y, gain1_ref[...], bias1_ref[...])
        h = _swiglu(x1, wg_ref, wu_ref, wd_ref)
        if pending is not None:
            p_rows, p_x1, p_h = pending
            o_ref[p_rows, :] = _layer_norm_rows(ALPHA * p_x1 + 0.5 * p_h, gain2_ref[...], bias2_ref[...])
        pending = (rows, x1, h)
    p_rows, p_x1, p_h = pending
    o_ref[p_rows, :] = _layer_norm_rows(ALPHA * p_x1 + 0.5 * p_h, gain2_ref[...], bias2_ref[...])


def _proj_ln_ffn_ln(attn2d, x2d, w_o, mixer_layer, ffn_weights, ln_params, layer):
    n_rows, d = x2d.shape
    wg, wu, wd = ffn_weights
    gains, biases = ln_params
    return pl.pallas_call(
        _proj_ln_ffn_ln_kernel,
        grid=(n_rows // ROW_TILE,),
        in_specs=[_row_spec(d), _row_spec(d), _stacked(w_o, (mixer_layer,))]
                 + [_stacked(p, (layer, 1)) for p in (gains, biases)]
                 + [_stacked(w, (layer, 1)) for w in (wg, wu, wd)]
                 + [_stacked(p, (layer, 2)) for p in (gains, biases)],
        out_specs=_row_spec(d),
        out_shape=jax.ShapeDtypeStruct((n_rows, d), _F32),
        compiler_params=_compiler_params(1),
        name="proj_ln_ffn_ln",
    )(attn2d, x2d, w_o, gains, biases, wg, wu, wd, gains, biases)


def _head_lane_masks(dtype):
    lane = lax.broadcasted_iota(jnp.int32, (1, HEAD_PAIR_WIDTH), 1)
    first = lane < HEAD_DIM
    return [first.astype(dtype), (~first).astype(dtype)], first


def _head_pair_specs(seq):
    def spec(offset):
        return pl.BlockSpec((None, seq, HEAD_PAIR_WIDTH), lambda b, hp: (b, 0, offset + hp))
    return [spec(0), spec(N_HEAD_PAIRS), spec(2 * N_HEAD_PAIRS)], spec(0)


def _band_block_order(n_blocks):
    n_short = min(LEFT_CONTEXT // A_Q_BLOCK, n_blocks)
    short, full = list(range(n_short)), list(range(n_short, n_blocks))
    order = []
    while full or short:
        order += full[:2]
        full = full[2:]
        order += short[:1]
        short = short[1:]
    return order


def _attn_a_kernel(q_ref, k_ref, v_ref, diag_ref, o_ref, qs_ref, bias_ref,
                   s0_ref, s1_ref, max0_ref, max1_ref, p0_ref, p1_ref, denom0_ref, denom1_ref):
    seq = q_ref.shape[0]
    head_masks, first_head = _head_lane_masks(_BF16)
    s_refs, max_refs = (s0_ref, s1_ref), (max0_ref, max1_ref)
    p_refs, denom_refs = (p0_ref, p1_ref), (denom0_ref, denom1_ref)
    stacked_rows = HEADS_PER_STEP * A_Q_BLOCK

    for qi in range(seq // A_Q_BLOCK):
        q2 = q_ref[qi * A_Q_BLOCK:(qi + 1) * A_Q_BLOCK, :]
        for h in range(HEADS_PER_STEP):
            r0 = (qi * HEADS_PER_STEP + h) * A_Q_BLOCK
            qs_ref[r0:r0 + A_Q_BLOCK, :] = q2 * head_masks[h]

    i = lax.broadcasted_iota(jnp.int32, (A_Q_BLOCK, A_KEY_WINDOW), 0)
    j = lax.broadcasted_iota(jnp.int32, (A_Q_BLOCK, A_KEY_WINDOW), 1)
    in_band = (j // CHUNK >= i // CHUNK) & (j // CHUNK <= i // CHUNK + LEFT_CHUNKS)
    diag_len = diag_ref.shape[-1]
    for h in range(HEADS_PER_STEP):
        rows = jnp.broadcast_to(diag_ref[h], (A_Q_BLOCK, diag_len))
        toeplitz = pltpu.roll(rows, diag_len - A_Q_BLOCK, 1, stride=1, stride_axis=0)
        bias_ref[h * A_Q_BLOCK:(h + 1) * A_Q_BLOCK, :] = jnp.where(
            in_band, toeplitz[:, :A_KEY_WINDOW], -jnp.inf)

    order = _band_block_order(seq // A_Q_BLOCK)

    def window(qi):
        stop = (qi + 1) * A_Q_BLOCK
        start = max(stop - A_KEY_WINDOW, 0)
        return slice(start, stop), stop - start

    def lane_tiled(per_row, n_keys):
        return jnp.concatenate([per_row] * (n_keys // HEAD_PAIR_WIDTH), axis=1)

    def scores(u):
        qi = order[u]
        keys, n_keys = window(qi)
        s = lax.dot_general(qs_ref[qi * stacked_rows:(qi + 1) * stacked_rows, :], k_ref[keys, :],
                            (((1,), (1,)), ((), ())), preferred_element_type=_F32)
        s = s + bias_ref[:, A_KEY_WINDOW - n_keys:]
        s_refs[u % 2][:, :n_keys] = s
        max_refs[u % 2][...] = jnp.broadcast_to(jnp.max(s, axis=-1, keepdims=True),
                                                (stacked_rows, HEAD_PAIR_WIDTH))

    def numerators(u):
        _, n_keys = window(order[u])
        p = jnp.exp(s_refs[u % 2][:, :n_keys] - lane_tiled(max_refs[u % 2][...], n_keys))
        denom_refs[u % 2][...] = jnp.broadcast_to(jnp.sum(p, axis=-1, keepdims=True),
                                                  (stacked_rows, HEAD_PAIR_WIDTH))
        p_refs[u % 2][:, :n_keys] = p.astype(_BF16)

    def values(u):
        qi = order[u]
        keys, n_keys = window(qi)
        o = jnp.dot(p_refs[u % 2][:, :n_keys], v_ref[keys, :], preferred_element_type=_F32)
        o = o / denom_refs[u % 2][...]
        o_ref[qi * A_Q_BLOCK:(qi + 1) * A_Q_BLOCK, :] = jnp.where(
            first_head, o[:A_Q_BLOCK], o[A_Q_BLOCK:]).astype(o_ref.dtype)

    n_units = len(order)
    for t in range(-2, n_units):
        if t + 2 < n_units:
            scores(t + 2)
        if 0 <= t + 1 < n_units:
            numerators(t + 1)
        if t >= 0:
            values(t)


def _bias_diagonals(rel_table):
    width = A_KEY_WINDOW + A_Q_BLOCK
    far = jnp.broadcast_to(rel_table[2 * REL_CLIP], (width - 2 * REL_CLIP, rel_table.shape[1]))
    near = rel_table[2 * REL_CLIP:0:-1]
    return jnp.concatenate([far, near], axis=0).T.reshape(rel_table.shape[1], 1, width).astype(_F32)


def _attn_a(qkv3d, bias_diagonals):
    batch, seq, _ = qkv3d.shape
    in_specs, out_spec = _head_pair_specs(seq)
    width = bias_diagonals.shape[-1]
    diag_spec = pl.BlockSpec((HEADS_PER_STEP, 1, width), lambda b, hp: (hp, 0, 0))
    stacked_rows = HEADS_PER_STEP * A_Q_BLOCK
    return pl.pallas_call(
        _attn_a_kernel,
        grid=(batch, N_HEAD_PAIRS),
        in_specs=in_specs + [diag_spec],
        out_specs=out_spec,
        out_shape=jax.ShapeDtypeStruct((batch, seq, D_MODEL), _BF16),
        scratch_shapes=[
            pltpu.VMEM((HEADS_PER_STEP * seq, HEAD_PAIR_WIDTH), _BF16),
            pltpu.VMEM((stacked_rows, A_KEY_WINDOW), _F32),
            pltpu.VMEM((stacked_rows, A_KEY_WINDOW), _F32),
            pltpu.VMEM((stacked_rows, A_KEY_WINDOW), _F32),
            pltpu.VMEM((stacked_rows, HEAD_PAIR_WIDTH), _F32),
            pltpu.VMEM((stacked_rows, HEAD_PAIR_WIDTH), _F32),
            pltpu.VMEM((stacked_rows, A_KEY_WINDOW), _BF16),
            pltpu.VMEM((stacked_rows, A_KEY_WINDOW), _BF16),
            pltpu.VMEM((stacked_rows, HEAD_PAIR_WIDTH), _F32),
            pltpu.VMEM((stacked_rows, HEAD_PAIR_WIDTH), _F32),
        ],
        compiler_params=_compiler_params(2),
        name="attn_band",
    )(qkv3d, qkv3d, qkv3d, bias_diagonals)


def _stick_units(seq):
    n = seq // B_KEY_BLOCK
    return [(g, kj, half) for kj in range(n - 1, -1, -1) for g in range(kj, n) for half in range(2)]


def _attn_b_kernel(q_ref, k_ref, v_ref, o_ref,
                   qs_ref, acc_ref, tail_ref, tri_ref,
                   drop_ref, keep_ref,
                   logit0_ref, logit1_ref, addend0_ref, addend1_ref, a0_ref, a1_ref):
    seq = q_ref.shape[0]
    head_masks, first_head = _head_lane_masks(_BF16)
    logit_refs, addend_refs, a_refs = (logit0_ref, logit1_ref), (addend0_ref, addend1_ref), (a0_ref, a1_ref)

    for qi in range(seq // B_Q_BLOCK):
        q2 = q_ref[qi * B_Q_BLOCK:(qi + 1) * B_Q_BLOCK, :]
        for h in range(HEADS_PER_STEP):
            r0 = (qi * HEADS_PER_STEP + h) * B_Q_BLOCK
            qs_ref[r0:r0 + B_Q_BLOCK, :] = q2 * head_masks[h]

    key_j = lax.broadcasted_iota(jnp.int32, (B_KEY_BLOCK, B_KEY_BLOCK), 0)
    key_s = lax.broadcasted_iota(jnp.int32, (B_KEY_BLOCK, B_KEY_BLOCK), 1)
    tri_ref[...] = (key_j > key_s).astype(_BF16)

    row = lax.broadcasted_iota(jnp.int32, (B_UNIT_ROWS, B_KEY_BLOCK), 0)
    col = lax.broadcasted_iota(jnp.int32, (B_UNIT_ROWS, B_KEY_BLOCK), 1)
    q_in_block = (row // (HEADS_PER_STEP * B_Q_BLOCK)) * B_Q_BLOCK + row % B_Q_BLOCK
    causal = col < q_in_block
    drop_ref[...] = jnp.where(causal, 0.0, -jnp.inf)
    keep_ref[...] = jnp.where(causal, 1.0, 0.0)
    sign_bit = jnp.uint32(0x80000000)

    units = _stick_units(seq)

    def parts(u):
        g, kj, half = units[u]
        half_rows = B_UNIT_ROWS // 2
        rows = slice(half * half_rows, (half + 1) * half_rows)
        if g != kj:
            return [(rows, B_KEY_BLOCK)], False
        return [(rows, (half + 1) * B_KEY_BLOCK // 2)], True

    def unit_rows(u, part_rows):
        base = units[u][0] * B_UNIT_ROWS
        return slice(base + part_rows.start, base + part_rows.stop)

    def unit_keys(u, n_keys):
        base = units[u][1] * B_KEY_BLOCK
        return slice(base, base + n_keys)

    def scores(u):
        pieces, diagonal = parts(u)
        for part_rows, n_keys in pieces:
            rows = unit_rows(u, part_rows)
            n_rows = part_rows.stop - part_rows.start
            y = lax.dot_general(qs_ref[rows, :], k_ref[unit_keys(u, n_keys), :],
                                (((1,), (1,)), ((), ())), preferred_element_type=_F32)
            neg_abs = lax.bitcast_convert_type(lax.bitcast_convert_type(y, jnp.uint32) | sign_bit, _F32)
            log_beta = jnp.minimum(y, 0.0) - jnp.log(1.0 + jnp.exp2(neg_abs)) * LOG2_E
            log_1m_beta = log_beta - y
            if diagonal:
                log_beta = log_beta + drop_ref[part_rows, :n_keys]
                log_1m_beta = log_1m_beta * keep_ref[part_rows, :n_keys]
            row_sum = jnp.broadcast_to(jnp.sum(log_1m_beta, axis=-1, keepdims=True),
                                       (n_rows, HEAD_PAIR_WIDTH))
            if diagonal:
                tail_ref[rows, :] = row_sum
                logit_refs[u % 2][part_rows, :n_keys] = log_beta
            else:
                tail = tail_ref[rows, :]
                tail_ref[rows, :] = tail + row_sum
                logit_refs[u % 2][part_rows, :n_keys] = log_beta + jnp.concatenate(
                    [tail] * (n_keys // HEAD_PAIR_WIDTH), axis=1)
            addend_refs[u % 2][part_rows, :n_keys] = log_1m_beta.astype(_BF16)

    def weights(u):
        for part_rows, n_keys in parts(u)[0]:
            later = jnp.dot(addend_refs[u % 2][part_rows, :n_keys], tri_ref[:n_keys, :n_keys],
                            preferred_element_type=_F32)
            a_refs[u % 2][part_rows, :n_keys] = jnp.exp2(
                logit_refs[u % 2][part_rows, :n_keys] + later).astype(_BF16)

    def values(u):
        pieces, diagonal = parts(u)
        for part_rows, n_keys in pieces:
            rows = unit_rows(u, part_rows)
            out = jnp.dot(a_refs[u % 2][part_rows, :n_keys], v_ref[unit_keys(u, n_keys), :],
                          preferred_element_type=_F32)
            if diagonal:
                acc_ref[rows, :] = out
            else:
                acc_ref[rows, :] += out

    n_units = len(units)
    for t in range(-2, n_units):
        if t + 2 < n_units:
            scores(t + 2)
        if 0 <= t + 1 < n_units:
            weights(t + 1)
        if t >= 0:
            values(t)

    for qi in range(seq // B_Q_BLOCK):
        r0 = qi * HEADS_PER_STEP * B_Q_BLOCK
        o_ref[qi * B_Q_BLOCK:(qi + 1) * B_Q_BLOCK, :] = jnp.where(
            first_head, acc_ref[r0:r0 + B_Q_BLOCK, :], acc_ref[r0 + B_Q_BLOCK:r0 + 2 * B_Q_BLOCK, :]
        ).astype(o_ref.dtype)


def _attn_b(qkv3d):
    batch, seq, _ = qkv3d.shape
    in_specs, out_spec = _head_pair_specs(seq)
    stacked = HEADS_PER_STEP * seq
    return pl.pallas_call(
        _attn_b_kernel,
        grid=(batch, N_HEAD_PAIRS),
        in_specs=in_specs,
        out_specs=out_spec,
        out_shape=jax.ShapeDtypeStruct((batch, seq, D_MODEL), _BF16),
        scratch_shapes=[
            pltpu.VMEM((stacked, HEAD_PAIR_WIDTH), _BF16),
            pltpu.VMEM((stacked, HEAD_PAIR_WIDTH), _F32),
            pltpu.VMEM((stacked, HEAD_PAIR_WIDTH), _F32),
            pltpu.VMEM((B_KEY_BLOCK, B_KEY_BLOCK), _BF16),
            pltpu.VMEM((B_UNIT_ROWS, B_KEY_BLOCK), _F32),
            pltpu.VMEM((B_UNIT_ROWS, B_KEY_BLOCK), _F32),
            pltpu.VMEM((B_UNIT_ROWS, B_KEY_BLOCK), _F32),
            pltpu.VMEM((B_UNIT_ROWS, B_KEY_BLOCK), _F32),
            pltpu.VMEM((B_UNIT_ROWS, B_KEY_BLOCK), _BF16),
            pltpu.VMEM((B_UNIT_ROWS, B_KEY_BLOCK), _BF16),
            pltpu.VMEM((B_UNIT_ROWS, B_KEY_BLOCK), _BF16),
            pltpu.VMEM((B_UNIT_ROWS, B_KEY_BLOCK), _BF16),
        ],
        compiler_params=_compiler_params(2),
        name="attn_stick",
    )(qkv3d, qkv3d, qkv3d)


def kernel(x, w_qkv_a, w_o_a, rel_bias, w_qkv_b, w_o_b, ffn_w_gate, ffn_w_up, ffn_w_down, ln_g, ln_b):
    batch, seq, d = x.shape
    assert d == D_MODEL and (batch * seq) % ROW_TILE == 0
    assert seq % B_KEY_BLOCK == 0 and seq % A_Q_BLOCK == 0

    ffn_weights = tuple(w.astype(_BF16) for w in (ffn_w_gate, ffn_w_up, ffn_w_down))
    ln_params = (ln_g.reshape(DEPTH, 3, 1, D_MODEL), ln_b.reshape(DEPTH, 3, 1, D_MODEL))
    w_qkv = (w_qkv_a.astype(_BF16), w_qkv_b.astype(_BF16))
    w_o = (w_o_a.astype(_BF16), w_o_b.astype(_BF16))
    bias_diagonals = _bias_diagonals(rel_bias)
    q_scales = (HEAD_DIM ** -0.5, HEAD_DIM ** -0.5 * LOG2_E)

    x2d = x.reshape(batch * seq, d)
    for layer in range(DEPTH):
        mixer, mixer_layer = layer % 2, layer // 2
        x2d, qkv = _ffn_ln_qkv(x2d, ffn_weights, ln_params, layer, w_qkv[mixer], mixer_layer,
                               q_scales[mixer])
        qkv = qkv.reshape(batch, seq, 3 * d)
        attn = _attn_a(qkv, bias_diagonals) if mixer == 0 else _attn_b(qkv)
        x2d = _proj_ln_ffn_ln(attn.reshape(batch * seq, d), x2d, w_o[mixer], mixer_layer,
                              ffn_weights, ln_params, layer)
    return x2d.reshape(batch, seq, d)
```

```python
import functools
import math

import jax
import jax.numpy as jnp
from jax import lax
from jax.experimental import pallas as pl
from jax.experimental.pallas import tpu as pltpu

D_MODEL = 1024
DEPTH = 4
N_HEADS = 16
HEAD_DIM = D_MODEL // N_HEADS
HEADS_PER_STEP = 2
HEAD_PAIR_WIDTH = HEADS_PER_STEP * HEAD_DIM
N_HEAD_PAIRS = N_HEADS // HEADS_PER_STEP
D_FF = 2816
CHUNK = 64
LEFT_CHUNKS = 8
LEFT_CONTEXT = LEFT_CHUNKS * CHUNK
REL_CLIP = 128
ALPHA = (2.0 * DEPTH) ** 0.25
LN_EPS = 1e-5
LOG2_E = math.log2(math.e)

ROW_SLICES = (256, 256)
FFN_QKV_ROW_TILE = 512
PROJ_FFN_ROW_TILE = 512
A_Q_BLOCK = 128
A_KEY_WINDOW = LEFT_CONTEXT + A_Q_BLOCK
B_Q_BLOCK = 128
B_KEY_BLOCK = 256
B_UNIT_ROWS = HEADS_PER_STEP * B_KEY_BLOCK
VMEM_LIMIT_BYTES = 56 * 1024 * 1024

_BF16 = jnp.bfloat16
_F32 = jnp.float32


def _compiler_params(n_grid_dims):
    return pltpu.CompilerParams(
        dimension_semantics=("arbitrary",) * n_grid_dims,
        vmem_limit_bytes=VMEM_LIMIT_BYTES,
    )


def _layer_norm_rows(y, gain, bias):
    mu = jnp.mean(y, axis=-1, keepdims=True)
    d = y - mu
    var = jnp.mean(d * d, axis=-1, keepdims=True)
    return d * lax.rsqrt(var + LN_EPS) * gain + bias


def _stacked(array, lead):
    tail = array.shape[len(lead):]
    index = tuple(lead) + (0,) * len(tail)
    return pl.BlockSpec((None,) * len(lead) + tail, lambda *_: index, pipeline_mode=pl.Buffered(1))


def _row_spec(tile, width):
    return pl.BlockSpec((tile, width), lambda i: (i, 0))


def _swiglu(x, wg_ref, wu_ref, wd_ref):
    xb = x.astype(_BF16)
    gate = jnp.dot(xb, wg_ref[...], preferred_element_type=_F32)
    up = jnp.dot(xb, wu_ref[...], preferred_element_type=_F32)
    hidden = (gate * jax.nn.sigmoid(gate) * up).astype(_BF16)
    return jnp.dot(hidden, wd_ref[...], preferred_element_type=_F32)


def _row_slices(tile):
    assert tile == sum(ROW_SLICES)
    starts = [sum(ROW_SLICES[:i]) for i in range(len(ROW_SLICES))]
    return [slice(s, s + n) for s, n in zip(starts, ROW_SLICES)]


def _ffn_ln_qkv_kernel(x_ref, wg_ref, wu_ref, wd_ref, gain_ref, bias_ref, wqkv_ref, x_out_ref, qkv_ref,
                       *, q_scale):
    d = x_ref.shape[-1]
    subtiles = _row_slices(x_ref.shape[0])

    def finish(rows, h):
        x1 = _layer_norm_rows(ALPHA * x_ref[rows, :] + 0.5 * h, gain_ref[...], bias_ref[...])
        x_out_ref[rows, :] = x1
        qkv = jnp.dot(x1.astype(_BF16), wqkv_ref[...], preferred_element_type=_F32)
        qkv_ref[rows, :d] = (qkv[:, :d] * q_scale).astype(qkv_ref.dtype)
        qkv_ref[rows, d:] = qkv[:, d:].astype(qkv_ref.dtype)

    pending = None
    for rows in subtiles:
        h = _swiglu(x_ref[rows, :], wg_ref, wu_ref, wd_ref)
        if pending is not None:
            finish(*pending)
        pending = (rows, h)
    finish(*pending)


def _ffn_ln_qkv(x2d, ffn_weights, ln_params, layer, w_qkv, mixer_layer, q_scale):
    n_rows, d = x2d.shape
    n_out = w_qkv.shape[-1]
    tile = FFN_QKV_ROW_TILE
    wg, wu, wd = ffn_weights
    gains, biases = ln_params
    return pl.pallas_call(
        functools.partial(_ffn_ln_qkv_kernel, q_scale=q_scale),
        grid=(n_rows // tile,),
        in_specs=[_row_spec(tile, d)] + [_stacked(w, (layer, 0)) for w in (wg, wu, wd)]
                 + [_stacked(p, (layer, 0)) for p in (gains, biases)]
                 + [_stacked(w_qkv, (mixer_layer,))],
        out_specs=[_row_spec(tile, d), _row_spec(tile, n_out)],
        out_shape=[jax.ShapeDtypeStruct((n_rows, d), _F32),
                   jax.ShapeDtypeStruct((n_rows, n_out), _BF16)],
        compiler_params=_compiler_params(1),
        name="ffn_ln_qkv",
    )(x2d, wg, wu, wd, gains, biases, w_qkv)


def _proj_ln_ffn_ln_kernel(a_ref, x_ref, wo_ref, gain1_ref, bias1_ref,
                           wg_ref, wu_ref, wd_ref, gain2_ref, bias2_ref, o_ref):
    subtiles = _row_slices(x_ref.shape[0])
    ys = [jnp.dot(a_ref[rows, :], wo_ref[...], preferred_element_type=_F32) for rows in subtiles]
    pending = None
    for rows, y in zip(subtiles, ys):
        x1 = _layer_norm_rows(ALPHA * x_ref[rows, :] + y, gain1_ref[...], bias1_ref[...])
        h = _swiglu(x1, wg_ref, wu_ref, wd_ref)
        if pending is not None:
            p_rows, p_x1, p_h = pending
            o_ref[p_rows, :] = _layer_norm_rows(ALPHA * p_x1 + 0.5 * p_h, gain2_ref[...], bias2_ref[...])
        pending = (rows, x1, h)
    p_rows, p_x1, p_h = pending
    o_ref[p_rows, :] = _layer_norm_rows(ALPHA * p_x1 + 0.5 * p_h, gain2_ref[...], bias2_ref[...])


def _proj_ln_ffn_ln(attn2d, x2d, w_o, mixer_layer, ffn_weights, ln_params, layer):
    n_rows, d = x2d.shape
    tile = PROJ_FFN_ROW_TILE
    wg, wu, wd = ffn_weights
    gains, biases = ln_params
    return pl.pallas_call(
        _proj_ln_ffn_ln_kernel,
        grid=(n_rows // tile,),
        in_specs=[_row_spec(tile, d), _row_spec(tile, d), _stacked(w_o, (mixer_layer,))]
                 + [_stacked(p, (layer, 1)) for p in (gains, biases)]
                 + [_stacked(w, (layer, 1)) for w in (wg, wu, wd)]
                 + [_stacked(p, (layer, 2)) for p in (gains, biases)],
        out_specs=_row_spec(tile, d),
        out_shape=jax.ShapeDtypeStruct((n_rows, d), _F32),
        compiler_params=_compiler_params(1),
        name="proj_ln_ffn_ln",
    )(attn2d, x2d, w_o, gains, biases, wg, wu, wd, gains, biases)


def _head_lane_masks(dtype):
    lane = lax.broadcasted_iota(jnp.int32, (1, HEAD_PAIR_WIDTH), 1)
    first = lane < HEAD_DIM
    return [first.astype(dtype), (~first).astype(dtype)], first


def _head_pair_specs(seq):
    def spec(offset):
        return pl.BlockSpec((None, seq, HEAD_PAIR_WIDTH), lambda hp, b: (b, 0, offset + hp))
    return [spec(0), spec(N_HEAD_PAIRS), spec(2 * N_HEAD_PAIRS)], spec(0)


def _band_block_order(n_blocks):
    n_short = min(LEFT_CONTEXT // A_Q_BLOCK, n_blocks)
    short, full = list(range(n_short)), list(range(n_short, n_blocks))
    order = []
    while full or short:
        order += full[:2]
        full = full[2:]
        order += short[:1]
        short = short[1:]
    return order


def _attn_a_kernel(q_ref, k_ref, v_ref, diag_ref, o_ref, qs_ref, bias_ref,
                   s0_ref, s1_ref, max0_ref, max1_ref, p0_ref, p1_ref, denom0_ref, denom1_ref):
    seq = q_ref.shape[0]
    head_masks, first_head = _head_lane_masks(_BF16)
    s_refs, max_refs = (s0_ref, s1_ref), (max0_ref, max1_ref)
    p_refs, denom_refs = (p0_ref, p1_ref), (denom0_ref, denom1_ref)
    stacked_rows = HEADS_PER_STEP * A_Q_BLOCK

    for qi in range(seq // A_Q_BLOCK):
        q2 = q_ref[qi * A_Q_BLOCK:(qi + 1) * A_Q_BLOCK, :]
        for h in range(HEADS_PER_STEP):
            r0 = (qi * HEADS_PER_STEP + h) * A_Q_BLOCK
            qs_ref[r0:r0 + A_Q_BLOCK, :] = q2 * head_masks[h]

    @pl.when(pl.program_id(1) == 0)
    def _():
        i = lax.broadcasted_iota(jnp.int32, (A_Q_BLOCK, A_KEY_WINDOW), 0)
        j = lax.broadcasted_iota(jnp.int32, (A_Q_BLOCK, A_KEY_WINDOW), 1)
        in_band = (j // CHUNK >= i // CHUNK) & (j // CHUNK <= i // CHUNK + LEFT_CHUNKS)
        diag_len = diag_ref.shape[-1]
        for h in range(HEADS_PER_STEP):
            rows = jnp.broadcast_to(diag_ref[h], (A_Q_BLOCK, diag_len))
            toeplitz = pltpu.roll(rows, diag_len - A_Q_BLOCK, 1, stride=1, stride_axis=0)
            bias_ref[h * A_Q_BLOCK:(h + 1) * A_Q_BLOCK, :] = jnp.where(
                in_band, toeplitz[:, :A_KEY_WINDOW], -jnp.inf)

    order = _band_block_order(seq // A_Q_BLOCK)

    def window(qi):
        stop = (qi + 1) * A_Q_BLOCK
        start = max(stop - A_KEY_WINDOW, 0)
        return slice(start, stop), stop - start

    def lane_tiled(per_row, n_keys):
        return jnp.concatenate([per_row] * (n_keys // HEAD_PAIR_WIDTH), axis=1)

    def scores(u):
        qi = order[u]
        keys, n_keys = window(qi)
        s = lax.dot_general(qs_ref[qi * stacked_rows:(qi + 1) * stacked_rows, :], k_ref[keys, :],
                            (((1,), (1,)), ((), ())), preferred_element_type=_F32)
        s = s + bias_ref[:, A_KEY_WINDOW - n_keys:]
        s_refs[u % 2][:, :n_keys] = s
        max_refs[u % 2][...] = jnp.broadcast_to(jnp.max(s, axis=-1, keepdims=True),
                                                (stacked_rows, HEAD_PAIR_WIDTH))

    def numerators(u):
        _, n_keys = window(order[u])
        p = jnp.exp(s_refs[u % 2][:, :n_keys] - lane_tiled(max_refs[u % 2][...], n_keys))
        denom_refs[u % 2][...] = jnp.broadcast_to(jnp.sum(p, axis=-1, keepdims=True),
                                                  (stacked_rows, HEAD_PAIR_WIDTH))
        p_refs[u % 2][:, :n_keys] = p.astype(_BF16)

    def values(u):
        qi = order[u]
        keys, n_keys = window(qi)
        o = jnp.dot(p_refs[u % 2][:, :n_keys], v_ref[keys, :], preferred_element_type=_F32)
        o = o / denom_refs[u % 2][...]
        o_ref[qi * A_Q_BLOCK:(qi + 1) * A_Q_BLOCK, :] = jnp.where(
            first_head, o[:A_Q_BLOCK], o[A_Q_BLOCK:]).astype(o_ref.dtype)

    n_units = len(order)
    for t in range(-2, n_units):
        if t + 2 < n_units:
            scores(t + 2)
        if 0 <= t + 1 < n_units:
            numerators(t + 1)
        if t >= 0:
            values(t)


def _bias_diagonals(rel_table):
    width = A_KEY_WINDOW + A_Q_BLOCK
    far = jnp.broadcast_to(rel_table[2 * REL_CLIP], (width - 2 * REL_CLIP, rel_table.shape[1]))
    near = rel_table[2 * REL_CLIP:0:-1]
    return jnp.concatenate([far, near], axis=0).T.reshape(rel_table.shape[1], 1, width).astype(_F32)


def _attn_a(qkv3d, bias_diagonals):
    batch, seq, _ = qkv3d.shape
    in_specs, out_spec = _head_pair_specs(seq)
    width = bias_diagonals.shape[-1]
    diag_spec = pl.BlockSpec((HEADS_PER_STEP, 1, width), lambda hp, b: (hp, 0, 0))
    stacked_rows = HEADS_PER_STEP * A_Q_BLOCK
    return pl.pallas_call(
        _attn_a_kernel,
        grid=(N_HEAD_PAIRS, batch),
        in_specs=in_specs + [diag_spec],
        out_specs=out_spec,
        out_shape=jax.ShapeDtypeStruct((batch, seq, D_MODEL), _BF16),
        scratch_shapes=[
            pltpu.VMEM((HEADS_PER_STEP * seq, HEAD_PAIR_WIDTH), _BF16),
            pltpu.VMEM((stacked_rows, A_KEY_WINDOW), _F32),
            pltpu.VMEM((stacked_rows, A_KEY_WINDOW), _F32),
            pltpu.VMEM((stacked_rows, A_KEY_WINDOW), _F32),
            pltpu.VMEM((stacked_rows, HEAD_PAIR_WIDTH), _F32),
            pltpu.VMEM((stacked_rows, HEAD_PAIR_WIDTH), _F32),
            pltpu.VMEM((stacked_rows, A_KEY_WINDOW), _BF16),
            pltpu.VMEM((stacked_rows, A_KEY_WINDOW), _BF16),
            pltpu.VMEM((stacked_rows, HEAD_PAIR_WIDTH), _F32),
            pltpu.VMEM((stacked_rows, HEAD_PAIR_WIDTH), _F32),
        ],
        compiler_params=_compiler_params(2),
        name="attn_band",
    )(qkv3d, qkv3d, qkv3d, bias_diagonals)


def _stick_units(seq):
    n = seq // B_KEY_BLOCK
    return [(g, kj) for kj in range(n - 1, -1, -1) for g in range(kj, n)]


def _attn_b_kernel(q_ref, k_ref, v_ref, o_ref,
                   qs_ref, acc_ref, tail_ref, tri_ref,
                   drop_ref, keep_ref,
                   logit0_ref, logit1_ref, addend0_ref, addend1_ref, a0_ref, a1_ref):
    seq = q_ref.shape[0]
    head_masks, first_head = _head_lane_masks(_BF16)
    logit_refs, addend_refs, a_refs = (logit0_ref, logit1_ref), (addend0_ref, addend1_ref), (a0_ref, a1_ref)

    for qi in range(seq // B_Q_BLOCK):
        q2 = q_ref[qi * B_Q_BLOCK:(qi + 1) * B_Q_BLOCK, :]
        for h in range(HEADS_PER_STEP):
            r0 = (qi * HEADS_PER_STEP + h) * B_Q_BLOCK
            qs_ref[r0:r0 + B_Q_BLOCK, :] = q2 * head_masks[h]

    key_j = lax.broadcasted_iota(jnp.int32, (B_KEY_BLOCK, B_KEY_BLOCK), 0)
    key_s = lax.broadcasted_iota(jnp.int32, (B_KEY_BLOCK, B_KEY_BLOCK), 1)
    tri_ref[...] = (key_j > key_s).astype(_BF16)

    row = lax.broadcasted_iota(jnp.int32, (B_UNIT_ROWS, B_KEY_BLOCK), 0)
    col = lax.broadcasted_iota(jnp.int32, (B_UNIT_ROWS, B_KEY_BLOCK), 1)
    q_in_block = (row // (HEADS_PER_STEP * B_Q_BLOCK)) * B_Q_BLOCK + row % B_Q_BLOCK
    causal = col < q_in_block
    drop_ref[...] = jnp.where(causal, 0.0, -jnp.inf)
    keep_ref[...] = jnp.where(causal, 1.0, 0.0)
    sign_bit = jnp.uint32(0x80000000)

    units = _stick_units(seq)

    def parts(u):
        g, kj = units[u]
        if g != kj:
            return [(slice(0, B_UNIT_ROWS), B_KEY_BLOCK)], False
        half = B_UNIT_ROWS // 2
        return [(slice(0, half), B_KEY_BLOCK // 2), (slice(half, B_UNIT_ROWS), B_KEY_BLOCK)], True

    def unit_rows(u, part_rows):
        base = units[u][0] * B_UNIT_ROWS
        return slice(base + part_rows.start, base + part_rows.stop)

    def unit_keys(u, n_keys):
        base = units[u][1] * B_KEY_BLOCK
        return slice(base, base + n_keys)

    def scores(u):
        pieces, diagonal = parts(u)
        for part_rows, n_keys in pieces:
            rows = unit_rows(u, part_rows)
            n_rows = part_rows.stop - part_rows.start
            y = lax.dot_general(qs_ref[rows, :], k_ref[unit_keys(u, n_keys), :],
                                (((1,), (1,)), ((), ())), preferred_element_type=_F32)
            neg_abs = lax.bitcast_convert_type(lax.bitcast_convert_type(y, jnp.uint32) | sign_bit, _F32)
            log_beta = jnp.minimum(y, 0.0) - jnp.log(1.0 + jnp.exp2(neg_abs)) * LOG2_E
            log_1m_beta = log_beta - y
            if diagonal:
                log_beta = log_beta + drop_ref[part_rows, :n_keys]
                log_1m_beta = log_1m_beta * keep_ref[part_rows, :n_keys]
            row_sum = jnp.broadcast_to(jnp.sum(log_1m_beta, axis=-1, keepdims=True),
                                       (n_rows, HEAD_PAIR_WIDTH))
            if diagonal:
                tail_ref[rows, :] = row_sum
                logit_refs[u % 2][part_rows, :n_keys] = log_beta
            else:
                tail = tail_ref[rows, :]
                tail_ref[rows, :] = tail + row_sum
                logit_refs[u % 2][part_rows, :n_keys] = log_beta + jnp.concatenate(
                    [tail] * (n_keys // HEAD_PAIR_WIDTH), axis=1)
            addend_refs[u % 2][part_rows, :n_keys] = log_1m_beta.astype(_BF16)

    def weights(u):
        for part_rows, n_keys in parts(u)[0]:
            later = jnp.dot(addend_refs[u % 2][part_rows, :n_keys], tri_ref[:n_keys, :n_keys],
                            preferred_element_type=_F32)
            a_refs[u % 2][part_rows, :n_keys] = jnp.exp2(
                logit_refs[u % 2][part_rows, :n_keys] + later).astype(_BF16)

    def values(u):
        pieces, diagonal = parts(u)
        for part_rows, n_keys in pieces:
            rows = unit_rows(u, part_rows)
            out = jnp.dot(a_refs[u % 2][part_rows, :n_keys], v_ref[unit_keys(u, n_keys), :],
                          preferred_element_type=_F32)
            if diagonal:
                acc_ref[rows, :] = out
            else:
                acc_ref[rows, :] += out

    n_units = len(units)
    for t in range(-2, n_units):
        if t + 2 < n_units:
            scores(t + 2)
        if 0 <= t + 1 < n_units:
            weights(t + 1)
        if t >= 0:
            values(t)

    for qi in range(seq // B_Q_BLOCK):
        r0 = qi * HEADS_PER_STEP * B_Q_BLOCK
        o_ref[qi * B_Q_BLOCK:(qi + 1) * B_Q_BLOCK, :] = jnp.where(
            first_head, acc_ref[r0:r0 + B_Q_BLOCK, :], acc_ref[r0 + B_Q_BLOCK:r0 + 2 * B_Q_BLOCK, :]
        ).astype(o_ref.dtype)


def _attn_b(qkv3d):
    batch, seq, _ = qkv3d.shape
    in_specs, out_spec = _head_pair_specs(seq)
    stacked = HEADS_PER_STEP * seq
    return pl.pallas_call(
        _attn_b_kernel,
        grid=(N_HEAD_PAIRS, batch),
        in_specs=in_specs,
        out_specs=out_spec,
        out_shape=jax.ShapeDtypeStruct((batch, seq, D_MODEL), _BF16),
        scratch_shapes=[
            pltpu.VMEM((stacked, HEAD_PAIR_WIDTH), _BF16),
            pltpu.VMEM((stacked, HEAD_PAIR_WIDTH), _F32),
            pltpu.VMEM((stacked, HEAD_PAIR_WIDTH), _F32),
            pltpu.VMEM((B_KEY_BLOCK, B_KEY_BLOCK), _BF16),
            pltpu.VMEM((B_UNIT_ROWS, B_KEY_BLOCK), _F32),
            pltpu.VMEM((B_UNIT_ROWS, B_KEY_BLOCK), _F32),
            pltpu.VMEM((B_UNIT_ROWS, B_KEY_BLOCK), _F32),
            pltpu.VMEM((B_UNIT_ROWS, B_KEY_BLOCK), _F32),
            pltpu.VMEM((B_UNIT_ROWS, B_KEY_BLOCK), _BF16),
            pltpu.VMEM((B_UNIT_ROWS, B_KEY_BLOCK), _BF16),
            pltpu.VMEM((B_UNIT_ROWS, B_KEY_BLOCK), _BF16),
            pltpu.VMEM((B_UNIT_ROWS, B_KEY_BLOCK), _BF16),
        ],
        compiler_params=_compiler_params(2),
        name="attn_stick",
    )(qkv3d, qkv3d, qkv3d)


def kernel(x, w_qkv_a, w_o_a, rel_bias, w_qkv_b, w_o_b, ffn_w_gate, ffn_w_up, ffn_w_down, ln_g, ln_b):
    batch, seq, d = x.shape
    assert d == D_MODEL and (batch * seq) % max(FFN_QKV_ROW_TILE, PROJ_FFN_ROW_TILE) == 0
    assert seq % B_KEY_BLOCK == 0 and seq % A_Q_BLOCK == 0

    ffn_weights = tuple(w.astype(_BF16) for w in (ffn_w_gate, ffn_w_up, ffn_w_down))
    ln_params = (ln_g.reshape(DEPTH, 3, 1, D_MODEL), ln_b.reshape(DEPTH, 3, 1, D_MODEL))
    w_qkv = (w_qkv_a.astype(_BF16), w_qkv_b.astype(_BF16))
    w_o = (w_o_a.astype(_BF16), w_o_b.astype(_BF16))
    bias_diagonals = _bias_diagonals(rel_bias)
    q_scales = (HEAD_DIM ** -0.5, HEAD_DIM ** -0.5 * LOG2_E)

    x2d = x.reshape(batch * seq, d)
    for layer in range(DEPTH):
        mixer, mixer_layer = layer % 2, layer // 2
        x2d, qkv = _ffn_ln_qkv(x2d, ffn_weights, ln_params, layer, w_qkv[mixer], mixer_layer,
                               q_scales[mixer])
        qkv = qkv.reshape(batch, seq, 3 * d)
        attn = _attn_a(qkv, bias_diagonals) if mixer == 0 else _attn_b(qkv)
        x2d = _proj_ln_ffn_ln(attn.reshape(batch * seq, d), x2d, w_o[mixer], mixer_layer,
                              ffn_weights, ln_params, layer)
    return x2d.reshape(batch, seq, d)
```

```python
import functools
import math

import jax
import jax.numpy as jnp
from jax import lax
from jax.experimental import pallas as pl
from jax.experimental.pallas import tpu as pltpu

D_MODEL = 1024
DEPTH = 4
N_HEADS = 16
HEAD_DIM = D_MODEL // N_HEADS
HEADS_PER_STEP = 2
HEAD_PAIR_WIDTH = HEADS_PER_STEP * HEAD_DIM
N_HEAD_PAIRS = N_HEADS // HEADS_PER_STEP
D_FF = 2816
CHUNK = 64
LEFT_CHUNKS = 8
LEFT_CONTEXT = LEFT_CHUNKS * CHUNK
REL_CLIP = 128
ALPHA = (2.0 * DEPTH) ** 0.25
LN_EPS = 1e-5
LOG2_E = math.log2(math.e)

ROW_SLICES = (256, 256)
FFN_QKV_ROW_TILE = 512
PROJ_FFN_ROW_TILE = 512
A_Q_BLOCK = 128
A_KEY_WINDOW = LEFT_CONTEXT + A_Q_BLOCK
B_Q_BLOCK = 128
B_KEY_BLOCK = 256
B_UNIT_ROWS = HEADS_PER_STEP * B_KEY_BLOCK
VMEM_LIMIT_BYTES = 56 * 1024 * 1024

_BF16 = jnp.bfloat16
_F32 = jnp.float32


def _compiler_params(n_grid_dims):
    return pltpu.CompilerParams(
        dimension_semantics=("arbitrary",) * n_grid_dims,
        vmem_limit_bytes=VMEM_LIMIT_BYTES,
    )


def _layer_norm_rows(y, gain, bias):
    mu = jnp.mean(y, axis=-1, keepdims=True)
    d = y - mu
    var = jnp.mean(d * d, axis=-1, keepdims=True)
    return d * lax.rsqrt(var + LN_EPS) * gain + bias


def _stacked(array, lead):
    tail = array.shape[len(lead):]
    index = tuple(lead) + (0,) * len(tail)
    return pl.BlockSpec((None,) * len(lead) + tail, lambda *_: index, pipeline_mode=pl.Buffered(1))


def _row_spec(tile, width):
    return pl.BlockSpec((tile, width), lambda i: (i, 0))


def _swiglu(x, wg_ref, wu_ref, wd_ref):
    xb = x.astype(_BF16)
    gate = jnp.dot(xb, wg_ref[...], preferred_element_type=_F32)
    up = jnp.dot(xb, wu_ref[...], preferred_element_type=_F32)
    hidden = (gate * jax.nn.sigmoid(gate) * up).astype(_BF16)
    return jnp.dot(hidden, wd_ref[...], preferred_element_type=_F32)


def _row_slices(tile):
    assert tile == sum(ROW_SLICES)
    starts = [sum(ROW_SLICES[:i]) for i in range(len(ROW_SLICES))]
    return [slice(s, s + n) for s, n in zip(starts, ROW_SLICES)]


def _ffn_ln_qkv_kernel(x_ref, wg_ref, wu_ref, wd_ref, gain_ref, bias_ref, wqkv_ref, x_out_ref, qkv_ref,
                       *, q_scale):
    d = x_ref.shape[-1]
    subtiles = _row_slices(x_ref.shape[0])

    def finish(rows, h):
        x1 = _layer_norm_rows(ALPHA * x_ref[rows, :] + 0.5 * h, gain_ref[...], bias_ref[...])
        x_out_ref[rows, :] = x1
        qkv = jnp.dot(x1.astype(_BF16), wqkv_ref[...], preferred_element_type=_F32)
        qkv_ref[rows, :d] = (qkv[:, :d] * q_scale).astype(qkv_ref.dtype)
        qkv_ref[rows, d:] = qkv[:, d:].astype(qkv_ref.dtype)

    pending = None
    for rows in subtiles:
        h = _swiglu(x_ref[rows, :], wg_ref, wu_ref, wd_ref)
        if pending is not None:
            finish(*pending)
        pending = (rows, h)
    finish(*pending)


def _ffn_ln_qkv(x2d, ffn_weights, ln_params, layer, w_qkv, mixer_layer, q_scale):
    n_rows, d = x2d.shape
    n_out = w_qkv.shape[-1]
    tile = FFN_QKV_ROW_TILE
    wg, wu, wd = ffn_weights
    gains, biases = ln_params
    return pl.pallas_call(
        functools.partial(_ffn_ln_qkv_kernel, q_scale=q_scale),
        grid=(n_rows // tile,),
        in_specs=[_row_spec(tile, d)] + [_stacked(w, (layer, 0)) for w in (wg, wu, wd)]
                 + [_stacked(p, (layer, 0)) for p in (gains, biases)]
                 + [_stacked(w_qkv, (mixer_layer,))],
        out_specs=[_row_spec(tile, d), _row_spec(tile, n_out)],
        out_shape=[jax.ShapeDtypeStruct((n_rows, d), _F32),
                   jax.ShapeDtypeStruct((n_rows, n_out), _BF16)],
        compiler_params=_compiler_params(1),
        name="ffn_ln_qkv",
    )(x2d, wg, wu, wd, gains, biases, w_qkv)


def _proj_ln_ffn_ln_kernel(a_ref, x_ref, wo_ref, gain1_ref, bias1_ref,
                           wg_ref, wu_ref, wd_ref, gain2_ref, bias2_ref, o_ref):
    subtiles = _row_slices(x_ref.shape[0])
    ys = [jnp.dot(a_ref[rows, :], wo_ref[...], preferred_element_type=_F32) for rows in subtiles]
    pending = None
    for rows, y in zip(subtiles, ys):
        x1 = _layer_norm_rows(ALPHA * x_ref[rows, :] + y, gain1_ref[...], bias1_ref[...])
        h = _swiglu(x1, wg_ref, wu_ref, wd_ref)
        if pending is not None:
            p_rows, p_x1, p_h = pending
            o_ref[p_rows, :] = _layer_norm_rows(ALPHA * p_x1 + 0.5 * p_h, gain2_ref[...], bias2_ref[...])
        pending = (rows, x1, h)
    p_rows, p_x1, p_h = pending
    o_ref[p_rows, :] = _layer_norm_rows(ALPHA * p_x1 + 0.5 * p_h, gain2_ref[...], bias2_ref[...])


def _proj_ln_ffn_ln(attn2d, x2d, w_o, mixer_layer, ffn_weights, ln_params, layer):
    n_rows, d = x2d.shape
    tile = PROJ_FFN_ROW_TILE
    wg, wu, wd = ffn_weights
    gains, biases = ln_params
    return pl.pallas_call(
        _proj_ln_ffn_ln_kernel,
        grid=(n_rows // tile,),
        in_specs=[_row_spec(tile, d), _row_spec(tile, d), _stacked(w_o, (mixer_layer,))]
                 + [_stacked(p, (layer, 1)) for p in (gains, biases)]
                 + [_stacked(w, (layer, 1)) for w in (wg, wu, wd)]
                 + [_stacked(p, (layer, 2)) for p in (gains, biases)],
        out_specs=_row_spec(tile, d),
        out_shape=jax.ShapeDtypeStruct((n_rows, d), _F32),
        compiler_params=_compiler_params(1),
        name="proj_ln_ffn_ln",
    )(attn2d, x2d, w_o, gains, biases, wg, wu, wd, gains, biases)


def _head_lane_masks(dtype):
    lane = lax.broadcasted_iota(jnp.int32, (1, HEAD_PAIR_WIDTH), 1)
    first = lane < HEAD_DIM
    return [first.astype(dtype), (~first).astype(dtype)], first


def _head_pair_specs(seq):
    def spec(offset):
        return pl.BlockSpec((None, seq, HEAD_PAIR_WIDTH), lambda hp, b: (b, 0, offset + hp))
    return [spec(0), spec(N_HEAD_PAIRS), spec(2 * N_HEAD_PAIRS)], spec(0)


def _band_block_order(n_blocks):
    n_short = min(LEFT_CONTEXT // A_Q_BLOCK, n_blocks)
    short, full = list(range(n_short)), list(range(n_short, n_blocks))
    return short[:n_short // 2] + full + short[n_short // 2:]


def _attn_a_kernel(q_ref, k_ref, v_ref, diag_ref, o_ref, qs_ref, bias_ref,
                   s0_ref, s1_ref, max0_ref, max1_ref, p0_ref, p1_ref, denom0_ref, denom1_ref):
    seq = q_ref.shape[0]
    head_masks, first_head = _head_lane_masks(_BF16)
    s_refs, max_refs = (s0_ref, s1_ref), (max0_ref, max1_ref)
    p_refs, denom_refs = (p0_ref, p1_ref), (denom0_ref, denom1_ref)
    stacked_rows = HEADS_PER_STEP * A_Q_BLOCK

    for qi in range(seq // A_Q_BLOCK):
        q2 = q_ref[qi * A_Q_BLOCK:(qi + 1) * A_Q_BLOCK, :]
        for h in range(HEADS_PER_STEP):
            r0 = (qi * HEADS_PER_STEP + h) * A_Q_BLOCK
            qs_ref[r0:r0 + A_Q_BLOCK, :] = q2 * head_masks[h]

    @pl.when(pl.program_id(1) == 0)
    def _():
        i = lax.broadcasted_iota(jnp.int32, (A_Q_BLOCK, A_KEY_WINDOW), 0)
        j = lax.broadcasted_iota(jnp.int32, (A_Q_BLOCK, A_KEY_WINDOW), 1)
        in_band = (j // CHUNK >= i // CHUNK) & (j // CHUNK <= i // CHUNK + LEFT_CHUNKS)
        diag_len = diag_ref.shape[-1]
        for h in range(HEADS_PER_STEP):
            rows = jnp.broadcast_to(diag_ref[h], (A_Q_BLOCK, diag_len))
            toeplitz = pltpu.roll(rows, diag_len - A_Q_BLOCK, 1, stride=1, stride_axis=0)
            bias_ref[h * A_Q_BLOCK:(h + 1) * A_Q_BLOCK, :] = jnp.where(
                in_band, toeplitz[:, :A_KEY_WINDOW], -jnp.inf)

    order = _band_block_order(seq // A_Q_BLOCK)

    def window(qi):
        stop = (qi + 1) * A_Q_BLOCK
        start = max(stop - A_KEY_WINDOW, 0)
        return slice(start, stop), stop - start

    def lane_tiled(per_row, n_keys):
        return jnp.concatenate([per_row] * (n_keys // HEAD_PAIR_WIDTH), axis=1)

    def scores(u):
        qi = order[u]
        keys, n_keys = window(qi)
        s = lax.dot_general(qs_ref[qi * stacked_rows:(qi + 1) * stacked_rows, :], k_ref[keys, :],
                            (((1,), (1,)), ((), ())), preferred_element_type=_F32)
        s = s + bias_ref[:, A_KEY_WINDOW - n_keys:]
        s_refs[u % 2][:, :n_keys] = s
        max_refs[u % 2][...] = jnp.broadcast_to(jnp.max(s, axis=-1, keepdims=True),
                                                (stacked_rows, HEAD_PAIR_WIDTH))

    def numerators(u):
        _, n_keys = window(order[u])
        p = jnp.exp(s_refs[u % 2][:, :n_keys] - lane_tiled(max_refs[u % 2][...], n_keys))
        denom_refs[u % 2][...] = jnp.broadcast_to(jnp.sum(p, axis=-1, keepdims=True),
                                                  (stacked_rows, HEAD_PAIR_WIDTH))
        p_refs[u % 2][:, :n_keys] = p.astype(_BF16)

    def values(u):
        qi = order[u]
        keys, n_keys = window(qi)
        o = jnp.dot(p_refs[u % 2][:, :n_keys], v_ref[keys, :], preferred_element_type=_F32)
        o = o / denom_refs[u % 2][...]
        o_ref[qi * A_Q_BLOCK:(qi + 1) * A_Q_BLOCK, :] = jnp.where(
            first_head, o[:A_Q_BLOCK], o[A_Q_BLOCK:]).astype(o_ref.dtype)

    n_units = len(order)
    for t in range(-2, n_units):
        if t + 2 < n_units:
            scores(t + 2)
        if 0 <= t + 1 < n_units:
            numerators(t + 1)
        if t >= 0:
            values(t)


def _bias_diagonals(rel_table):
    width = A_KEY_WINDOW + A_Q_BLOCK
    far = jnp.broadcast_to(rel_table[2 * REL_CLIP], (width - 2 * REL_CLIP, rel_table.shape[1]))
    near = rel_table[2 * REL_CLIP:0:-1]
    return jnp.concatenate([far, near], axis=0).T.reshape(rel_table.shape[1], 1, width).astype(_F32)


def _attn_a(qkv3d, bias_diagonals):
    batch, seq, _ = qkv3d.shape
    in_specs, out_spec = _head_pair_specs(seq)
    width = bias_diagonals.shape[-1]
    diag_spec = pl.BlockSpec((HEADS_PER_STEP, 1, width), lambda hp, b: (hp, 0, 0))
    stacked_rows = HEADS_PER_STEP * A_Q_BLOCK
    return pl.pallas_call(
        _attn_a_kernel,
        grid=(N_HEAD_PAIRS, batch),
        in_specs=in_specs + [diag_spec],
        out_specs=out_spec,
        out_shape=jax.ShapeDtypeStruct((batch, seq, D_MODEL), _BF16),
        scratch_shapes=[
            pltpu.VMEM((HEADS_PER_STEP * seq, HEAD_PAIR_WIDTH), _BF16),
            pltpu.VMEM((stacked_rows, A_KEY_WINDOW), _F32),
            pltpu.VMEM((stacked_rows, A_KEY_WINDOW), _F32),
            pltpu.VMEM((stacked_rows, A_KEY_WINDOW), _F32),
            pltpu.VMEM((stacked_rows, HEAD_PAIR_WIDTH), _F32),
            pltpu.VMEM((stacked_rows, HEAD_PAIR_WIDTH), _F32),
            pltpu.VMEM((stacked_rows, A_KEY_WINDOW), _BF16),
            pltpu.VMEM((stacked_rows, A_KEY_WINDOW), _BF16),
            pltpu.VMEM((stacked_rows, HEAD_PAIR_WIDTH), _F32),
            pltpu.VMEM((stacked_rows, HEAD_PAIR_WIDTH), _F32),
        ],
        compiler_params=_compiler_params(2),
        name="attn_band",
    )(qkv3d, qkv3d, qkv3d, bias_diagonals)


def _stick_units(seq):
    n = seq // B_KEY_BLOCK
    return [(g, kj) for g in range(n - 1, -1, -1) for kj in range(g, -1, -1)]


def _attn_b_kernel(q_ref, k_ref, v_ref, o_ref,
                   qs_ref, acc_ref, tail_ref, tri_ref,
                   drop_ref, keep_ref,
                   logit0_ref, logit1_ref, addend0_ref, addend1_ref, a0_ref, a1_ref):
    seq = q_ref.shape[0]
    head_masks, first_head = _head_lane_masks(_BF16)
    logit_refs, addend_refs, a_refs = (logit0_ref, logit1_ref), (addend0_ref, addend1_ref), (a0_ref, a1_ref)

    for qi in range(seq // B_Q_BLOCK):
        q2 = q_ref[qi * B_Q_BLOCK:(qi + 1) * B_Q_BLOCK, :]
        for h in range(HEADS_PER_STEP):
            r0 = (qi * HEADS_PER_STEP + h) * B_Q_BLOCK
            qs_ref[r0:r0 + B_Q_BLOCK, :] = q2 * head_masks[h]

    key_j = lax.broadcasted_iota(jnp.int32, (B_KEY_BLOCK, B_KEY_BLOCK), 0)
    key_s = lax.broadcasted_iota(jnp.int32, (B_KEY_BLOCK, B_KEY_BLOCK), 1)
    tri_ref[...] = (key_j > key_s).astype(_BF16)

    row = lax.broadcasted_iota(jnp.int32, (B_UNIT_ROWS, B_KEY_BLOCK), 0)
    col = lax.broadcasted_iota(jnp.int32, (B_UNIT_ROWS, B_KEY_BLOCK), 1)
    q_in_block = (row // (HEADS_PER_STEP * B_Q_BLOCK)) * B_Q_BLOCK + row % B_Q_BLOCK
    causal = col < q_in_block
    drop_ref[...] = jnp.where(causal, 0.0, -jnp.inf)
    keep_ref[...] = jnp.where(causal, 1.0, 0.0)
    sign_bit = jnp.uint32(0x80000000)

    units = _stick_units(seq)

    def parts(u):
        g, kj = units[u]
        if g != kj:
            return [(slice(0, B_UNIT_ROWS), B_KEY_BLOCK)], False
        half = B_UNIT_ROWS // 2
        return [(slice(0, half), B_KEY_BLOCK // 2), (slice(half, B_UNIT_ROWS), B_KEY_BLOCK)], True

    def unit_rows(u, part_rows):
        base = units[u][0] * B_UNIT_ROWS
        return slice(base + part_rows.start, base + part_rows.stop)

    def unit_keys(u, n_keys):
        base = units[u][1] * B_KEY_BLOCK
        return slice(base, base + n_keys)

    def scores(u):
        pieces, diagonal = parts(u)
        for part_rows, n_keys in pieces:
            rows = unit_rows(u, part_rows)
            n_rows = part_rows.stop - part_rows.start
            y = lax.dot_general(qs_ref[rows, :], k_ref[unit_keys(u, n_keys), :],
                                (((1,), (1,)), ((), ())), preferred_element_type=_F32)
            neg_abs = lax.bitcast_convert_type(lax.bitcast_convert_type(y, jnp.uint32) | sign_bit, _F32)
            log_beta = jnp.minimum(y, 0.0) - jnp.log(1.0 + jnp.exp2(neg_abs)) * LOG2_E
            log_1m_beta = log_beta - y
            if diagonal:
                log_beta = log_beta + drop_ref[part_rows, :n_keys]
                log_1m_beta = log_1m_beta * keep_ref[part_rows, :n_keys]
            row_sum = jnp.broadcast_to(jnp.sum(log_1m_beta, axis=-1, keepdims=True),
                                       (n_rows, HEAD_PAIR_WIDTH))
            if diagonal:
                tail_ref[rows, :] = row_sum
                logit_refs[u % 2][part_rows, :n_keys] = log_beta
            else:
                tail = tail_ref[rows, :]
                tail_ref[rows, :] = tail + row_sum
                logit_refs[u % 2][part_rows, :n_keys] = log_beta + jnp.concatenate(
                    [tail] * (n_keys // HEAD_PAIR_WIDTH), axis=1)
            addend_refs[u % 2][part_rows, :n_keys] = log_1m_beta.astype(_BF16)

    def weights(u):
        for part_rows, n_keys in parts(u)[0]:
            later = jnp.dot(addend_refs[u % 2][part_rows, :n_keys], tri_ref[:n_keys, :n_keys],
                            preferred_element_type=_F32)
            a_refs[u % 2][part_rows, :n_keys] = jnp.exp2(
                logit_refs[u % 2][part_rows, :n_keys] + later).astype(_BF16)

    def values(u):
        pieces, diagonal = parts(u)
        for part_rows, n_keys in pieces:
            rows = unit_rows(u, part_rows)
            out = jnp.dot(a_refs[u % 2][part_rows, :n_keys], v_ref[unit_keys(u, n_keys), :],
                          preferred_element_type=_F32)
            if diagonal:
                acc_ref[rows, :] = out
            else:
                acc_ref[rows, :] += out

    n_units = len(units)
    for t in range(-2, n_units):
        if t + 2 < n_units:
            scores(t + 2)
        if 0 <= t + 1 < n_units:
            weights(t + 1)
        if t >= 0:
            values(t)

    for qi in range(seq // B_Q_BLOCK):
        r0 = qi * HEADS_PER_STEP * B_Q_BLOCK
        o_ref[qi * B_Q_BLOCK:(qi + 1) * B_Q_BLOCK, :] = jnp.where(
            first_head, acc_ref[r0:r0 + B_Q_BLOCK, :], acc_ref[r0 + B_Q_BLOCK:r0 + 2 * B_Q_BLOCK, :]
        ).astype(o_ref.dtype)


def _attn_b(qkv3d):
    batch, seq, _ = qkv3d.shape
    in_specs, out_spec = _head_pair_specs(seq)
    stacked = HEADS_PER_STEP * seq
    return pl.pallas_call(
        _attn_b_kernel,
        grid=(N_HEAD_PAIRS, batch),
        in_specs=in_specs,
        out_specs=out_spec,
        out_shape=jax.ShapeDtypeStruct((batch, seq, D_MODEL), _BF16),
        scratch_shapes=[
            pltpu.VMEM((stacked, HEAD_PAIR_WIDTH), _BF16),
            pltpu.VMEM((stacked, HEAD_PAIR_WIDTH), _F32),
            pltpu.VMEM((stacked, HEAD_PAIR_WIDTH), _F32),
            pltpu.VMEM((B_KEY_BLOCK, B_KEY_BLOCK), _BF16),
            pltpu.VMEM((B_UNIT_ROWS, B_KEY_BLOCK), _F32),
            pltpu.VMEM((B_UNIT_ROWS, B_KEY_BLOCK), _F32),
            pltpu.VMEM((B_UNIT_ROWS, B_KEY_BLOCK), _F32),
            pltpu.VMEM((B_UNIT_ROWS, B_KEY_BLOCK), _F32),
            pltpu.VMEM((B_UNIT_ROWS, B_KEY_BLOCK), _BF16),
            pltpu.VMEM((B_UNIT_ROWS, B_KEY_BLOCK), _BF16),
            pltpu.VMEM((B_UNIT_ROWS, B_KEY_BLOCK), _BF16),
            pltpu.VMEM((B_UNIT_ROWS, B_KEY_BLOCK), _BF16),
        ],
        compiler_params=_compiler_params(2),
        name="attn_stick",
    )(qkv3d, qkv3d, qkv3d)


def kernel(x, w_qkv_a, w_o_a, rel_bias, w_qkv_b, w_o_b, ffn_w_gate, ffn_w_up, ffn_w_down, ln_g, ln_b):
    batch, seq, d = x.shape
    assert d == D_MODEL and (batch * seq) % max(FFN_QKV_ROW_TILE, PROJ_FFN_ROW_TILE) == 0
    assert seq % B_KEY_BLOCK == 0 and seq % A_Q_BLOCK == 0

    ffn_weights = tuple(w.astype(_BF16) for w in (ffn_w_gate, ffn_w_up, ffn_w_down))
    ln_params = (ln_g.reshape(DEPTH, 3, 1, D_MODEL), ln_b.reshape(DEPTH, 3, 1, D_MODEL))
    w_qkv = (w_qkv_a.astype(_BF16), w_qkv_b.astype(_BF16))
    w_o = (w_o_a.astype(_BF16), w_o_b.astype(_BF16))
    bias_diagonals = _bias_diagonals(rel_bias)
    q_scales = (HEAD_DIM ** -0.5, HEAD_DIM ** -0.5 * LOG2_E)

    x2d = x.reshape(batch * seq, d)
    for layer in range(DEPTH):
        mixer, mixer_layer = layer % 2, layer // 2
        x2d, qkv = _ffn_ln_qkv(x2d, ffn_weights, ln_params, layer, w_qkv[mixer], mixer_layer,
                               q_scales[mixer])
        qkv = qkv.reshape(batch, seq, 3 * d)
        attn = _attn_a(qkv, bias_diagonals) if mixer == 0 else _attn_b(qkv)
        x2d = _proj_ln_ffn_ln(attn.reshape(batch * seq, d), x2d, w_o[mixer], mixer_layer,
                              ffn_weights, ln_params, layer)
    return x2d.reshape(batch, seq, d)
```
